```python
import math, functools
import jax, jax.numpy as jnp
from jax import lax
import numpy as np

D_MODEL = 1024
BATCH = 8
SEQ = 4096
DEPTH = 1
DEC_BATCH = 32
DEC_SEQ = 4
PAST_LEN = 16384
PAGE_SIZE = 128

SSD_EXPAND = 2
D_INNER = SSD_EXPAND * D_MODEL
SSD_HEAD_DIM = 64
SSD_HEADS = D_INNER // SSD_HEAD_DIM
SSD_GROUPS = 8
SSD_HPG = SSD_HEADS // SSD_GROUPS
D_STATE = 128
CONV_W = 4
CONV_DIM = D_INNER + 2 * SSD_GROUPS * D_STATE
SSD_CHUNK = 128
SB_HEADS = 16
SB_HEAD_DIM = 64
SB_WIDTH = SB_HEADS * SB_HEAD_DIM
SB_SCALE = SB_HEAD_DIM ** -0.5
Q_BLOCK = 128
D_FF = -(-8 * D_MODEL // (3 * 256)) * 256
PLE_DIM = 256
LN_EPS = 1e-5
RMS_EPS = 1e-5
DN_ALPHA = (2 * DEPTH) ** 0.25
DN_BETA = (8 * DEPTH) ** -0.25
_IN_SIZES = (D_INNER, CONV_DIM, SSD_HEADS, SB_WIDTH, SB_WIDTH, SB_WIDTH, D_MODEL, D_MODEL)
IN_COLS = sum(_IN_SIZES)
IN_SPLITS = tuple(sum(_IN_SIZES[:i + 1]) for i in range(len(_IN_SIZES) - 1))

kernel_name = "hybrid_ssd_stickbreak_deepnorm_step"

F32 = jnp.float32


def layer_norm(x, g, b):
    xf = x.astype(F32)
    mu = xf.mean(-1, keepdims=True)
    var = jnp.square(xf - mu).mean(-1, keepdims=True)
    return ((xf - mu) * lax.rsqrt(var + LN_EPS) * g + b).astype(x.dtype)


def gated_group_rmsnorm(y, z, g):
    bt, t, _ = y.shape
    u = (y.astype(F32) * jax.nn.silu(z.astype(F32))).reshape(bt, t, SSD_GROUPS, -1)
    u = u * lax.rsqrt(jnp.mean(u * u, axis=-1, keepdims=True) + RMS_EPS)
    return u.reshape(bt, t, D_INNER) * g


def causal_dwconv(xbc, conv_state, w, b):
    xpad = jnp.concatenate([conv_state.astype(xbc.dtype), xbc], axis=1)
    y = lax.conv_general_dilated(xpad, w[:, None, :].astype(xbc.dtype), window_strides=(1,),
                                 padding="VALID", dimension_numbers=("NWC", "WIO", "NWC"),
                                 feature_group_count=CONV_DIM)
    return y + b, xpad[:, xpad.shape[1] - (CONV_W - 1):]


def ssd_chunked(x, dt, a, bm, cm, h0):
    bt, t = x.shape[:2]
    L = SSD_CHUNK if t % SSD_CHUNK == 0 else t
    nc = t // L

    def to_chunks(u):
        return jnp.moveaxis(u.reshape((bt, nc, L) + u.shape[2:]), 1, 0)

    xs = (to_chunks(x.reshape(bt, t, SSD_GROUPS, SSD_HPG, SSD_HEAD_DIM).astype(F32)),
          to_chunks(dt.reshape(bt, t, SSD_GROUPS, SSD_HPG).astype(F32)),
          to_chunks(bm.astype(F32)), to_chunks(cm.astype(F32)))
    a_gr = a.reshape(SSD_GROUPS, SSD_HPG)
    causal = jnp.tril(jnp.ones((L, L), bool))[None, :, :, None, None]

    def step(h, inp):
        xc, dtc, bc, cc = inp
        acum = jnp.cumsum(dtc * a_gr, axis=1)
        seg = acum[:, :, None] - acum[:, None]
        decay = jnp.exp(jnp.where(causal, seg, -jnp.inf))
        cb = jnp.einsum("blgn,bsgn->blsg", cc, bc)
        y = jnp.einsum("blsg,blsgr,bsgrp->blgrp", cb, decay, xc * dtc[..., None])
        y = y + jnp.einsum("blgn,bgrpn->blgrp", cc, h) * jnp.exp(acum)[..., None]
        last = acum[:, -1]
        w_in = jnp.exp(last[:, None] - acum) * dtc
        h = h * jnp.exp(last)[..., None, None] + jnp.einsum("bsgn,bsgrp->bgrpn", bc, xc * w_in[..., None])
        return h, y

    h_init = h0.reshape(bt, SSD_GROUPS, SSD_HPG, SSD_HEAD_DIM, D_STATE).astype(F32)
    h, ys = lax.scan(step, h_init, xs)
    y = jnp.moveaxis(ys, 0, 1).reshape(bt, t, SSD_HEADS, SSD_HEAD_DIM)
    return y, h.reshape(bt, SSD_HEADS, SSD_HEAD_DIM, D_STATE)


def sb_step(acc, logsum, q, k, v, bias, qpos, kpos):
    z = jnp.einsum("bqhd,bkhd->bhqk", q, k, preferred_element_type=F32) * SB_SCALE
    z = z + bias.astype(F32)[None, :, None, None]
    valid = kpos[None, :] < qpos[:, None]
    l1mb = jnp.where(valid, jax.nn.log_sigmoid(-z), 0.0)
    suffix = lax.cumsum(l1mb, axis=3, reverse=True) - l1mb + logsum[..., None]
    a = jnp.where(valid, jnp.exp(jax.nn.log_sigmoid(z) + suffix), 0.0)
    acc = acc + jnp.einsum("bhqk,bkhd->bhqd", a, v.astype(F32))
    return acc, logsum + l1mb.sum(-1)


def sb_prompt(q, k, v, bias):
    bt, s = q.shape[:2]
    nb = s // Q_BLOCK

    def one_block(i):
        q0 = i * Q_BLOCK
        qb = lax.dynamic_slice_in_dim(q, q0, Q_BLOCK, axis=1)
        qpos = q0 + jnp.arange(Q_BLOCK)

        def body(step, carry):
            k0 = (i - step) * Q_BLOCK
            kb = lax.dynamic_slice_in_dim(k, k0, Q_BLOCK, axis=1)
            vb = lax.dynamic_slice_in_dim(v, k0, Q_BLOCK, axis=1)
            return sb_step(carry[0], carry[1], qb, kb, vb, bias, qpos, k0 + jnp.arange(Q_BLOCK))

        init = (jnp.zeros((bt, SB_HEADS, Q_BLOCK, SB_HEAD_DIM), F32), jnp.zeros((bt, SB_HEADS, Q_BLOCK), F32))
        acc, _ = lax.fori_loop(0, i + 1, body, init)
        return acc

    out = lax.map(one_block, jnp.arange(nb))
    return out.transpose(1, 0, 3, 2, 4).reshape(bt, s, SB_WIDTH)


def sb_sample(q, k, v, bias, cache_k, cache_v, page_table, layer):
    bt, t = q.shape[:2]
    n_pages = page_table.shape[1]
    qpos = n_pages * PAGE_SIZE + jnp.arange(t)
    init = (jnp.zeros((bt, SB_HEADS, t, SB_HEAD_DIM), F32), jnp.zeros((bt, SB_HEADS, t), F32))
    carry = sb_step(init[0], init[1], q, k, v, bias, qpos, qpos)

    def body(c, xs):
        pidx, phys = xs
        kb = cache_k[layer, phys]
        vb = cache_v[layer, phys]
        return sb_step(c[0], c[1], q, kb, vb, bias, qpos, pidx * PAGE_SIZE + jnp.arange(PAGE_SIZE)), None

    (acc, _), _ = lax.scan(body, carry, (jnp.arange(n_pages)[::-1], page_table.T[::-1]))
    return acc.transpose(0, 2, 1, 3).reshape(bt, t, SB_WIDTH)


def trunk_layer(x, pe, conv_state, ssm_state, sb_attend, w):
    bt, t, _ = x.shape
    proj = x @ w["w_in"]
    zs, xbc_raw, dt_raw, q, k, v, ga, gb = jnp.split(proj, IN_SPLITS, axis=-1)
    xbc, new_conv = causal_dwconv(xbc_raw, conv_state, w["conv_w"], w["conv_b"])
    xbc = jax.nn.silu(xbc)
    xs, bm, cm = jnp.split(xbc, [D_INNER, D_INNER + SSD_GROUPS * D_STATE], axis=-1)
    xs = xs.reshape(bt, t, SSD_HEADS, SSD_HEAD_DIM)
    dt = jax.nn.softplus(dt_raw.astype(F32) + w["dt_bias"])
    a = -jnp.exp(w["a_log"].astype(F32))
    y, new_ssm = ssd_chunked(xs, dt, a, bm.reshape(bt, t, SSD_GROUPS, D_STATE),
                             cm.reshape(bt, t, SSD_GROUPS, D_STATE), ssm_state)
    y = y + w["d_skip"][:, None] * xs.astype(F32)
    y_ssd = gated_group_rmsnorm(y.reshape(bt, t, D_INNER), zs, w["ssd_norm_g"]).astype(x.dtype)
    k = k.reshape(bt, t, SB_HEADS, SB_HEAD_DIM)
    v = v.reshape(bt, t, SB_HEADS, SB_HEAD_DIM)
    o_sb = sb_attend(q.reshape(bt, t, SB_HEADS, SB_HEAD_DIM), k, v, w["sb_bias"]).astype(x.dtype)
    merged = jax.nn.sigmoid(ga) * (y_ssd @ w["w_ssd_out"]) + jax.nn.sigmoid(gb) * (o_sb @ w["w_sb_out"])
    h = layer_norm(DN_ALPHA * x + merged @ w["w_out"], w["ln1_g"], w["ln1_b"])
    g_, u_ = jnp.split(h @ w["w_gu"], 2, axis=-1)
    ffn = (jax.nn.silu(g_) * u_) @ w["w_down"]
    ple = jax.nn.sigmoid(h @ w["w_ple_gate"]) * (pe @ w["w_ple_proj"])
    out = layer_norm(DN_ALPHA * h + ffn + ple, w["ln2_g"], w["ln2_b"])
    return out, new_conv, new_ssm, k, v


def setup_inputs(seed: int = 0) -> dict:
    key = jax.random.key(seed)
    ks = jax.random.split(key, 32)
    n_pages = PAST_LEN // PAGE_SIZE
    n_used = DEC_BATCH * n_pages
    n_pool = n_used + n_used // 4

    def nrm(k, shape, scale):
        return jax.random.normal(k, shape, F32) * scale

    page_table = jax.random.permutation(ks[6], n_pool)[:n_used].reshape(DEC_BATCH, n_pages).astype(jnp.int32)
    dt0 = jnp.exp(jax.random.uniform(ks[12], (DEPTH, SSD_HEADS), F32, math.log(1e-3), math.log(1e-1)))
    return {
        "x_prompt": nrm(ks[0], (BATCH, SEQ, D_MODEL), 1.0),
        "x_sample": nrm(ks[1], (DEC_BATCH, DEC_SEQ, D_MODEL), 1.0),
        "cache_k": nrm(ks[2], (DEPTH, n_pool, PAGE_SIZE, SB_HEADS, SB_HEAD_DIM), 1.0),
        "cache_v": nrm(ks[3], (DEPTH, n_pool, PAGE_SIZE, SB_HEADS, SB_HEAD_DIM), 1.0),
        "state_conv": nrm(ks[4], (DEPTH, DEC_BATCH, CONV_W - 1, CONV_DIM), 1.0),
        "state_ssm": nrm(ks[5], (DEPTH, DEC_BATCH, SSD_HEADS, SSD_HEAD_DIM, D_STATE), 0.5),
        "page_table": page_table,
        "p_prompt": nrm(ks[7], (DEPTH, BATCH, SEQ, PLE_DIM), 1.0),
        "p_sample": nrm(ks[8], (DEPTH, DEC_BATCH, DEC_SEQ, PLE_DIM), 1.0),
        "w_in": nrm(ks[9], (DEPTH, D_MODEL, IN_COLS), D_MODEL ** -0.5),
        "conv_w": nrm(ks[10], (DEPTH, CONV_W, CONV_DIM), CONV_W ** -0.5),
        "conv_b": nrm(ks[11], (DEPTH, CONV_DIM), 0.02),
        "dt_bias": dt0 + jnp.log(-jnp.expm1(-dt0)),
        "a_log": jnp.log(jax.random.uniform(ks[13], (DEPTH, SSD_HEADS), F32, 1.0, 16.0)),
        "d_skip": 1.0 + nrm(ks[14], (DEPTH, SSD_HEADS), 0.02),
        "ssd_norm_g": 1.0 + nrm(ks[15], (DEPTH, D_INNER), 0.02),
        "sb_bias": jax.random.uniform(ks[27], (DEPTH, SB_HEADS), F32, -7.0, -5.0),
        "w_ssd_out": nrm(ks[16], (DEPTH, D_INNER, D_MODEL), DN_BETA * D_INNER ** -0.5),
        "w_sb_out": nrm(ks[17], (DEPTH, SB_WIDTH, D_MODEL), DN_BETA * SB_WIDTH ** -0.5),
        "w_out": nrm(ks[18], (DEPTH, D_MODEL, D_MODEL), DN_BETA * D_MODEL ** -0.5),
        "ln1_g": 1.0 + nrm(ks[19], (DEPTH, D_MODEL), 0.02),
        "ln1_b": nrm(ks[20], (DEPTH, D_MODEL), 0.02),
        "w_gu": nrm(ks[21], (DEPTH, D_MODEL, 2 * D_FF), D_MODEL ** -0.5),
        "w_down": nrm(ks[22], (DEPTH, D_FF, D_MODEL), DN_BETA * D_FF ** -0.5),
        "w_ple_gate": nrm(ks[23], (DEPTH, D_MODEL, D_MODEL), D_MODEL ** -0.5),
        "w_ple_proj": nrm(ks[24], (DEPTH, PLE_DIM, D_MODEL), DN_BETA * PLE_DIM ** -0.5),
        "ln2_g": 1.0 + nrm(ks[25], (DEPTH, D_MODEL), 0.02),
        "ln2_b": nrm(ks[26], (DEPTH, D_MODEL), 0.02),
    }


def reference(x_prompt, x_sample, cache_k, cache_v, state_conv, state_ssm, page_table, p_prompt, p_sample,
              w_in, conv_w, conv_b, dt_bias, a_log, d_skip, ssd_norm_g, sb_bias, w_ssd_out, w_sb_out, w_out,
              ln1_g, ln1_b, w_gu, w_down, w_ple_gate, w_ple_proj, ln2_g, ln2_b):
    xp, xs = x_prompt, x_sample
    kp_l, vp_l, cp_l, sp_l, ks_l, vs_l, cs_l, ss_l = [], [], [], [], [], [], [], []
    for l in range(DEPTH):
        w = dict(w_in=w_in[l], conv_w=conv_w[l], conv_b=conv_b[l], dt_bias=dt_bias[l], a_log=a_log[l],
                 d_skip=d_skip[l], ssd_norm_g=ssd_norm_g[l], sb_bias=sb_bias[l], w_ssd_out=w_ssd_out[l],
                 w_sb_out=w_sb_out[l], w_out=w_out[l], ln1_g=ln1_g[l], ln1_b=ln1_b[l], w_gu=w_gu[l],
                 w_down=w_down[l], w_ple_gate=w_ple_gate[l], w_ple_proj=w_ple_proj[l], ln2_g=ln2_g[l],
                 ln2_b=ln2_b[l])
        bp = xp.shape[0]
        xp, cp, sp, kp, vp = trunk_layer(
            xp, p_prompt[l], jnp.zeros((bp, CONV_W - 1, CONV_DIM), xp.dtype),
            jnp.zeros((bp, SSD_HEADS, SSD_HEAD_DIM, D_STATE), F32), sb_prompt, w)
        sb_fn = functools.partial(sb_sample, cache_k=cache_k, cache_v=cache_v, page_table=page_table, layer=l)
        xs, cs, ss, kn, vn = trunk_layer(xs, p_sample[l], state_conv[l], state_ssm[l], sb_fn, w)
        kp_l.append(kp); vp_l.append(vp); cp_l.append(cp); sp_l.append(sp)
        ks_l.append(kn); vs_l.append(vn); cs_l.append(cs); ss_l.append(ss)
    return (xp, xs, jnp.stack(kp_l), jnp.stack(vp_l), jnp.stack(cp_l), jnp.stack(sp_l),
            jnp.stack(ks_l), jnp.stack(vs_l), jnp.stack(cs_l), jnp.stack(ss_l))
```

```python
import functools

import jax
import jax.numpy as jnp
from jax import lax
from jax.experimental import pallas as pl
from jax.experimental.pallas import tpu as pltpu

F32 = jnp.float32
BF16 = jnp.bfloat16

D_MODEL = 1024
D_INNER = 2048
SSD_HEADS = 32
SSD_GROUPS = 8
SSD_HPG = SSD_HEADS // SSD_GROUPS
SSD_HEAD_DIM = 64
D_STATE = 128
CONV_W = 4
CONV_DIM = D_INNER + 2 * SSD_GROUPS * D_STATE
CHUNK = 128
SB_HEADS = 16
SB_HEAD_DIM = 64
SB_WIDTH = SB_HEADS * SB_HEAD_DIM
SB_SCALE = SB_HEAD_DIM ** -0.5
PAGE = 128
D_FF = 2816
PLE_DIM = 256
LN_EPS = 1e-5
RMS_EPS = 1e-5
DEPTH = 1
DN_ALPHA = (2 * DEPTH) ** 0.25

LANES = 128
SUBLANES = 8
VMEM_LIMIT = 56 * 1024 * 1024

_IN_SIZES = (D_INNER, CONV_DIM, SSD_HEADS, SB_WIDTH, SB_WIDTH, SB_WIDTH, D_MODEL, D_MODEL)


def _dot(a, b):
    return jnp.dot(a, b, preferred_element_type=F32)


def _dot_nt(a, b):
    return lax.dot_general(a, b, (((1,), (1,)), ((), ())), preferred_element_type=F32)


def _softplus(x):
    return jnp.maximum(x, 0.0) + jnp.log1p(jnp.exp(-jnp.abs(x)))


def _silu(x):
    return x * jax.nn.sigmoid(x)


def _split3(x):
    hi = x.astype(BF16)
    r = x - hi.astype(F32)
    mid = r.astype(BF16)
    lo = (r - mid.astype(F32)).astype(BF16)
    return hi, mid, lo


def _layer_norm(x, g, b):
    mu = jnp.mean(x, axis=-1, keepdims=True)
    d = x - mu
    var = jnp.mean(d * d, axis=-1, keepdims=True)
    return d * lax.rsqrt(var + LN_EPS) * g + b


def _full(shape):
    n = len(shape)
    return pl.BlockSpec(shape, lambda *_: (0,) * n)


def _ssd_kernel(x_ref, cs_ref, s0_ref, wz_ref, wx_ref, wdt_ref, cw_ref, cb_ref, dtb_ref, alog_ref,
                dsk_ref, g_ref, y_ref, co_ref, so_ref, xpad, act, *, n_valid_last):
    i = pl.program_id(1)
    n_i = pl.num_programs(1)
    L = CHUNK

    @pl.when(i == 0)
    def _():
        xpad[0:SUBLANES, :] = cs_ref[...]
        so_ref[...] = s0_ref[...]

    xb = x_ref[...].astype(BF16)
    xpad[SUBLANES:SUBLANES + L, :] = _dot(xb, wx_ref[...])

    for c in range(0, CONV_DIM, 512):
        sl = slice(c, c + 512)
        conv = cb_ref[:, sl]
        for j in range(CONV_W):
            r0 = SUBLANES - (CONV_W - 1) + j
            conv = conv + cw_ref[j:j + 1, sl] * xpad[r0:r0 + L, sl]
        act[:, sl] = _silu(conv)

    @pl.when(i == n_i - 1)
    def _():
        co_ref[...] = xpad[n_valid_last:n_valid_last + SUBLANES, :]

    xpad[0:SUBLANES, :] = xpad[L:L + SUBLANES, :]

    row = lax.broadcasted_iota(jnp.int32, (L, LANES), 0)
    col = lax.broadcasted_iota(jnp.int32, (L, LANES), 1)
    dt = _softplus(_dot(xb, wdt_ref[...]) + dtb_ref[...])
    limit = jnp.where(i == n_i - 1, n_valid_last, L)
    dt = jnp.where(row < limit, dt, 0.0)
    d_a = dt * (-jnp.exp(alog_ref[...]))
    causal = row >= col
    tril = causal.astype(BF16)
    hi, mid, lo = _split3(d_a)
    acum = _dot(tril, hi) + _dot(tril, mid) + _dot(tril, lo)
    acum_t = acum.T
    e_acum = jnp.exp(acum)
    w_in = jnp.exp(acum[L - 1:L, :] - acum) * dt
    e_last_t = jnp.exp(acum_t[:, L - 1:L])
    lo_half = col < SSD_HEAD_DIM
    gw = SSD_HPG * SSD_HEAD_DIM

    for g in range(SSD_GROUPS):
        b_g = act[:, D_INNER + D_STATE * g:D_INNER + D_STATE * (g + 1)].astype(BF16)
        c0 = D_INNER + SSD_GROUPS * D_STATE + D_STATE * g
        c_g = act[:, c0:c0 + D_STATE].astype(BF16)
        cb = _dot_nt(c_g, b_g)
        h_g = so_ref[gw * g:gw * (g + 1), :]
        inter = _dot_nt(c_g, h_g.astype(BF16))
        z_g = _dot(xb, wz_ref[:, gw * g:gw * (g + 1)])
        us, xws = [], []
        for jj in range(2):
            h0 = SSD_HPG * g + 2 * jj
            h1 = h0 + 1
            x0 = gw * g + LANES * jj
            x2 = act[:, x0:x0 + LANES]
            ms = []
            for h in (h0, h1):
                seg = acum[:, h:h + 1] - acum_t[h:h + 1, :]
                dec = jnp.exp(jnp.where(causal, seg, -jnp.inf))
                ms.append((cb * dec).astype(BF16))
            m_cat = jnp.concatenate(ms, axis=1)
            xdt = x2 * jnp.where(lo_half, dt[:, h0:h0 + 1], dt[:, h1:h1 + 1])
            x_stack = jnp.concatenate([jnp.where(lo_half, xdt, 0.0).astype(BF16),
                                       jnp.where(lo_half, 0.0, xdt).astype(BF16)], axis=0)
            y2 = _dot(m_cat, x_stack)
            e2 = jnp.where(lo_half, e_acum[:, h0:h0 + 1], e_acum[:, h1:h1 + 1])
            y2 = y2 + inter[:, LANES * jj:LANES * (jj + 1)] * e2 + dsk_ref[:, x0:x0 + LANES] * x2
            us.append(y2 * _silu(z_g[:, LANES * jj:LANES * (jj + 1)]))
            xws.append(x2 * jnp.where(lo_half, w_in[:, h0:h0 + 1], w_in[:, h1:h1 + 1]))
        ss = (jnp.sum(us[0] * us[0], axis=-1, keepdims=True)
              + jnp.sum(us[1] * us[1], axis=-1, keepdims=True))
        scale = lax.rsqrt(ss * (1.0 / gw) + RMS_EPS)
        for jj in range(2):
            x0 = gw * g + LANES * jj
            y_ref[:, x0:x0 + LANES] = (us[jj] * scale * g_ref[:, x0:x0 + LANES]).astype(y_ref.dtype)
        xw_t = jnp.concatenate(xws, axis=1).T
        e_rows = jnp.concatenate(
            [jnp.broadcast_to(e_last_t[SSD_HPG * g + r:SSD_HPG * g + r + 1, :], (SSD_HEAD_DIM, D_STATE))
             for r in range(SSD_HPG)], axis=0)
        so_ref[gw * g:gw * (g + 1), :] = h_g * e_rows + _dot(xw_t.astype(BF16), b_g)


def _ssd_branch(x, conv_state8, ssm0, w, n_valid_last):
    bt, t, _ = x.shape
    n_c = t // CHUNK
    kern = functools.partial(_ssd_kernel, n_valid_last=n_valid_last)
    return pl.pallas_call(
        kern,
        grid=(bt, n_c),
        in_specs=[
            pl.BlockSpec((None, CHUNK, D_MODEL), lambda b, i: (b, i, 0)),
            pl.BlockSpec((None, SUBLANES, CONV_DIM), lambda b, i: (b, 0, 0)),
            pl.BlockSpec((None, D_INNER, D_STATE), lambda b, i: (b, 0, 0)),
            _full((D_MODEL, D_INNER)),
            _full((D_MODEL, CONV_DIM)),
            _full((D_MODEL, LANES)),
            _full((CONV_W, CONV_DIM)),
            _full((1, CONV_DIM)),
            _full((1, LANES)),
            _full((1, LANES)),
            _full((1, D_INNER)),
            _full((1, D_INNER)),
        ],
        out_specs=[
            pl.BlockSpec((None, CHUNK, D_INNER), lambda b, i: (b, i, 0)),
            pl.BlockSpec((None, SUBLANES, CONV_DIM), lambda b, i: (b, 0, 0)),
            pl.BlockSpec((None, D_INNER, D_STATE), lambda b, i: (b, 0, 0)),
        ],
        out_shape=[
            jax.ShapeDtypeStruct((bt, t, D_INNER), BF16),
            jax.ShapeDtypeStruct((bt, SUBLANES, CONV_DIM), F32),
            jax.ShapeDtypeStruct((bt, D_INNER, D_STATE), F32),
        ],
        scratch_shapes=[
            pltpu.VMEM((SUBLANES + CHUNK, CONV_DIM), F32),
            pltpu.VMEM((CHUNK, CONV_DIM), F32),
        ],
        compiler_params=pltpu.CompilerParams(
            dimension_semantics=("parallel", "arbitrary"), vmem_limit_bytes=VMEM_LIMIT),
        name="ssd_branch",
    )(x, conv_state8, ssm0, w["w_z"], w["w_xbc"], w["w_dt"], w["conv_w"], w["conv_b"], w["dt_bias"],
      w["a_log"], w["d_skip"], w["ssd_norm_g"])


def _qkv_kernel(x_ref, w_ref, q_ref, k_ref, v_ref, kb_ref, vb_ref):
    xb = x_ref[...].astype(BF16)
    q_ref[...] = (_dot(xb, w_ref[:, 0:SB_WIDTH]) * SB_SCALE).astype(BF16)
    k = _dot(xb, w_ref[:, SB_WIDTH:2 * SB_WIDTH])
    k_ref[...] = k
    kb_ref[...] = k.astype(BF16)
    v = _dot(xb, w_ref[:, 2 * SB_WIDTH:3 * SB_WIDTH])
    v_ref[...] = v
    vb_ref[...] = v.astype(BF16)


def _qkv_proj(x2d, w_qkv, tm):
    t = x2d.shape[0]
    row = lambda i: (i, 0)
    return pl.pallas_call(
        _qkv_kernel,
        grid=(t // tm,),
        in_specs=[pl.BlockSpec((tm, D_MODEL), row), _full((D_MODEL, 3 * SB_WIDTH))],
        out_specs=[pl.BlockSpec((tm, SB_WIDTH), row)] * 5,
        out_shape=[
            jax.ShapeDtypeStruct((t, SB_WIDTH), BF16),
            jax.ShapeDtypeStruct((t, SB_WIDTH), F32),
            jax.ShapeDtypeStruct((t, SB_WIDTH), F32),
            jax.ShapeDtypeStruct((t, SB_WIDTH), BF16),
            jax.ShapeDtypeStruct((t, SB_WIDTH), BF16),
        ],
        compiler_params=pltpu.CompilerParams(
            dimension_semantics=("parallel",), vmem_limit_bytes=VMEM_LIMIT),
        name="qkv_proj",
    )(x2d, w_qkv)


def _suffix_weights(t):
    r = lax.broadcasted_iota(jnp.int32, (t, t), 0)
    c = lax.broadcasted_iota(jnp.int32, (t, t), 1)
    u = (r > c).astype(BF16)
    return jnp.concatenate([u, u], axis=0)


def _sb_block(z, log_sum, u2, valid):
    sp = _softplus(z)
    l1 = -sp
    if valid is not None:
        l1 = jnp.where(valid, l1, 0.0)
    hi = l1.astype(BF16)
    lo = (l1 - hi.astype(F32)).astype(BF16)
    suffix = _dot(jnp.concatenate([hi, lo], axis=1), u2)
    a = jnp.exp((z - sp) + suffix + log_sum)
    if valid is not None:
        a = jnp.where(valid, a, 0.0)
    return a, log_sum + jnp.sum(l1, axis=-1, keepdims=True)


def _sbp_kernel(bias_ref, q_ref, k_ref, v_ref, o_ref):
    hp = pl.program_id(1)
    qi = pl.program_id(2)
    t = q_ref.shape[0]
    b0 = bias_ref[2 * hp]
    b1 = bias_ref[2 * hp + 1]
    lane = lax.broadcasted_iota(jnp.int32, (t, LANES), 1)
    m0 = (lane < SB_HEAD_DIM).astype(BF16)
    m1 = (lane >= SB_HEAD_DIM).astype(BF16)
    q2 = q_ref[...]
    qs = jnp.concatenate([q2 * m0, q2 * m1], axis=0)
    u2 = _suffix_weights(t)
    r = lax.broadcasted_iota(jnp.int32, (t, t), 0)
    c = lax.broadcasted_iota(jnp.int32, (t, t), 1)
    diag_valid = c < r

    def tile(kt, carry, valid):
        acc, ls0, ls1 = carry
        k0 = pl.multiple_of(kt * t, t)
        k2 = k_ref[pl.ds(k0, t), :]
        v2 = v_ref[pl.ds(k0, t), :]
        z = _dot_nt(qs, k2)
        a0, ls0 = _sb_block(z[0:t] + b0, ls0, u2, valid)
        a1, ls1 = _sb_block(z[t:2 * t] + b1, ls1, u2, valid)
        a_cat = jnp.concatenate([a0.astype(BF16), a1.astype(BF16)], axis=1)
        v_stack = jnp.concatenate([v2 * m0, v2 * m1], axis=0)
        return acc + _dot(a_cat, v_stack), ls0, ls1

    zero_col = jnp.zeros((t, 1), F32)
    carry = tile(qi, (jnp.zeros((t, LANES), F32), zero_col, zero_col), diag_valid)
    acc, _, _ = lax.fori_loop(0, qi, lambda s, cr: tile(qi - 1 - s, cr, None), carry)
    o_ref[...] = acc.astype(o_ref.dtype)


def _sb_prompt(q, kb, vb, sb_bias, tq=128):
    b, s, _ = q.shape
    n_hp = SB_WIDTH // LANES
    grid_spec = pltpu.PrefetchScalarGridSpec(
        num_scalar_prefetch=1,
        grid=(b, n_hp, s // tq),
        in_specs=[
            pl.BlockSpec((None, tq, LANES), lambda bi, hp, qi, bias: (bi, qi, hp)),
            pl.BlockSpec((None, s, LANES), lambda bi, hp, qi, bias: (bi, 0, hp)),
            pl.BlockSpec((None, s, LANES), lambda bi, hp, qi, bias: (bi, 0, hp)),
        ],
        out_specs=pl.BlockSpec((None, tq, LANES), lambda bi, hp, qi, bias: (bi, qi, hp)),
    )
    return pl.pallas_call(
        _sbp_kernel,
        grid_spec=grid_spec,
        out_shape=jax.ShapeDtypeStruct((b, s, SB_WIDTH), BF16),
        compiler_params=pltpu.CompilerParams(
            dimension_semantics=("parallel", "parallel", "arbitrary"), vmem_limit_bytes=VMEM_LIMIT),
        name="sb_prompt",
    )(sb_bias, q, kb, vb)


def _sbs_kernel(pt_ref, q_ref, kn_ref, vn_ref, kp_ref, vp_ref, bias_ref, o_ref, acc, ls, *, n_tok):
    j = pl.program_id(1)
    n_j = pl.num_programs(1)
    n_rows = q_ref.shape[0]
    row = lax.broadcasted_iota(jnp.int32, (n_rows, SB_WIDTH), 0)
    lane = lax.broadcasted_iota(jnp.int32, (n_rows, SB_WIDTH), 1)
    head_mask = (row & (SB_HEADS - 1)) == (lane >> 6)
    q_bd = jnp.where(head_mask, q_ref[...].astype(F32), 0.0).astype(BF16)
    u2 = _suffix_weights(PAGE)

    def block(kb, vb, valid):
        z = _dot_nt(q_bd, kb) + bias_ref[...]
        a, new_ls = _sb_block(z, ls[...], u2, valid)
        acc[...] += _dot(a.astype(BF16), vb)
        ls[...] = new_ls

    @pl.when(j == 0)
    def _():
        acc[...] = jnp.zeros_like(acc)
        ls[...] = jnp.zeros_like(ls)
        tok = lax.broadcasted_iota(jnp.int32, (n_rows, PAGE), 0) >> 4
        key = lax.broadcasted_iota(jnp.int32, (n_rows, PAGE), 1)
        block(kn_ref[...], vn_ref[...], key < tok)

    block(kp_ref[...].astype(BF16), vp_ref[...].astype(BF16), None)

    @pl.when(j == n_j - 1)
    def _():
        own = jnp.where(head_mask, acc[...], 0.0)
        o_ref[...] = own.reshape(n_tok, SB_HEADS, SB_WIDTH).sum(axis=1)


def _sb_sample(q, k_new, v_new, cache_k, cache_v, page_table, sb_bias):
    bt, n_tok, _ = q.shape
    n_pages = page_table.shape[1]
    n_rows = n_tok * SB_HEADS
    q_rep = jnp.repeat(q, SB_HEADS, axis=1)
    pad = ((0, 0), (0, PAGE - n_tok), (0, 0))
    kn = jnp.pad(k_new, pad)
    vn = jnp.pad(v_new, pad)
    bias_rows = jnp.broadcast_to(jnp.tile(sb_bias, n_tok)[:, None], (n_rows, PAGE)).astype(F32)
    page = lambda b, j, pt: (pt[b, n_pages - 1 - j], 0, 0)
    per_b = lambda b, j, pt: (b, 0, 0)
    grid_spec = pltpu.PrefetchScalarGridSpec(
        num_scalar_prefetch=1,
        grid=(bt, n_pages),
        in_specs=[
            pl.BlockSpec((None, n_rows, SB_WIDTH), per_b),
            pl.BlockSpec((None, PAGE, SB_WIDTH), per_b),
            pl.BlockSpec((None, PAGE, SB_WIDTH), per_b),
            pl.BlockSpec((None, PAGE, SB_WIDTH), page),
            pl.BlockSpec((None, PAGE, SB_WIDTH), page),
            pl.BlockSpec((n_rows, PAGE), lambda b, j, pt: (0, 0)),
        ],
        out_specs=pl.BlockSpec((None, n_tok, SB_WIDTH), per_b),
        scratch_shapes=[pltpu.VMEM((n_rows, SB_WIDTH), F32), pltpu.VMEM((n_rows, 1), F32)],
    )
    return pl.pallas_call(
        functools.partial(_sbs_kernel, n_tok=n_tok),
        grid_spec=grid_spec,
        out_shape=jax.ShapeDtypeStruct((bt, n_tok, SB_WIDTH), F32),
        compiler_params=pltpu.CompilerParams(
            dimension_semantics=("parallel", "arbitrary"), vmem_limit_bytes=VMEM_LIMIT),
        name="sb_sample",
    )(page_table, q_rep, kn, vn, cache_k, cache_v, bias_rows)


def _merge_kernel(x_ref, y_ref, o_ref, wg_ref, wso_ref, wbo_ref, wo_ref, g_ref, b_ref, h_ref):
    x = x_ref[...]
    xb = x.astype(BF16)
    ga = _dot(xb, wg_ref[:, 0:D_MODEL])
    gb = _dot(xb, wg_ref[:, D_MODEL:2 * D_MODEL])
    p_ssd = _dot(y_ref[...].astype(BF16), wso_ref[...])
    p_sb = _dot(o_ref[...].astype(BF16), wbo_ref[...])
    merged = jax.nn.sigmoid(ga) * p_ssd + jax.nn.sigmoid(gb) * p_sb
    pre = DN_ALPHA * x + _dot(merged.astype(BF16), wo_ref[...])
    h_ref[...] = _layer_norm(pre, g_ref[...], b_ref[...])


def _merge_ln(x2d, y_ssd, o_sb, w, tm):
    t = x2d.shape[0]
    row = lambda i: (i, 0)
    return pl.pallas_call(
        _merge_kernel,
        grid=(t // tm,),
        in_specs=[
            pl.BlockSpec((tm, D_MODEL), row),
            pl.BlockSpec((tm, D_INNER), row),
            pl.BlockSpec((tm, SB_WIDTH), row),
            _full((D_MODEL, 2 * D_MODEL)),
            _full((D_INNER, D_MODEL)),
            _full((SB_WIDTH, D_MODEL)),
            _full((D_MODEL, D_MODEL)),
            _full((1, D_MODEL)),
            _full((1, D_MODEL)),
        ],
        out_specs=pl.BlockSpec((tm, D_MODEL), row),
        out_shape=jax.ShapeDtypeStruct((t, D_MODEL), F32),
        compiler_params=pltpu.CompilerParams(
            dimension_semantics=("parallel",), vmem_limit_bytes=VMEM_LIMIT),
        name="merge_ln",
    )(x2d, y_ssd, o_sb, w["w_gate"], w["w_ssd_out"], w["w_sb_out"], w["w_out"], w["ln1_g"], w["ln1_b"])


def _ffn_kernel(h_ref, pe_ref, wgu_ref, wd_ref, wpg_ref, wpp_ref, g_ref, b_ref, o_ref):
    h = h_ref[...]
    hb = h.astype(BF16)
    g_ = _dot(hb, wgu_ref[:, 0:D_FF])
    u_ = _dot(hb, wgu_ref[:, D_FF:2 * D_FF])
    ffn = _dot((_silu(g_) * u_).astype(BF16), wd_ref[...])
    ple = jax.nn.sigmoid(_dot(hb, wpg_ref[...])) * _dot(pe_ref[...].astype(BF16), wpp_ref[...])
    o_ref[...] = _layer_norm(DN_ALPHA * h + ffn + ple, g_ref[...], b_ref[...])


def _ffn_ln(h2d, pe2d, w, tm):
    t = h2d.shape[0]
    row = lambda i: (i, 0)
    once = pl.Buffered(1)
    return pl.pallas_call(
        _ffn_kernel,
        grid=(t // tm,),
        in_specs=[
            pl.BlockSpec((tm, D_MODEL), row),
            pl.BlockSpec((tm, PLE_DIM), row),
            pl.BlockSpec((D_MODEL, 2 * D_FF), lambda i: (0, 0), pipeline_mode=once),
            pl.BlockSpec((D_FF, D_MODEL), lambda i: (0, 0), pipeline_mode=once),
            pl.BlockSpec((D_MODEL, D_MODEL), lambda i: (0, 0), pipeline_mode=once),
            pl.BlockSpec((PLE_DIM, D_MODEL), lambda i: (0, 0), pipeline_mode=once),
            _full((1, D_MODEL)),
            _full((1, D_MODEL)),
        ],
        out_specs=pl.BlockSpec((tm, D_MODEL), row),
        out_shape=jax.ShapeDtypeStruct((t, D_MODEL), F32),
        compiler_params=pltpu.CompilerParams(
            dimension_semantics=("parallel",), vmem_limit_bytes=VMEM_LIMIT),
        name="ffn_ln",
    )(h2d, pe2d, w["w_gu"], w["w_down"], w["w_ple_gate"], w["w_ple_proj"], w["ln2_g"], w["ln2_b"])


def _prep_weights(w_in, conv_w, conv_b, dt_bias, a_log, d_skip, ssd_norm_g, w_ssd_out, w_sb_out, w_out,
                  ln1_g, ln1_b, w_gu, w_down, w_ple_gate, w_ple_proj, ln2_g, ln2_b):
    offs = [0]
    for s in _IN_SIZES:
        offs.append(offs[-1] + s)
    seg = lambda k: w_in[:, offs[k]:offs[k + 1]]
    lane_pad = lambda a: jnp.pad(a, ((0, 0), (0, LANES - a.shape[1])))
    return dict(
        w_z=seg(0).astype(BF16),
        w_xbc=seg(1).astype(BF16),
        w_dt=lane_pad(seg(2)).astype(BF16),
        w_qkv=jnp.concatenate([seg(3), seg(4), seg(5)], axis=1).astype(BF16),
        w_gate=jnp.concatenate([seg(6), seg(7)], axis=1).astype(BF16),
        conv_w=conv_w, conv_b=conv_b[None, :],
        dt_bias=lane_pad(dt_bias[None, :]), a_log=lane_pad(a_log[None, :]),
        d_skip=jnp.repeat(d_skip, SSD_HEAD_DIM)[None, :], ssd_norm_g=ssd_norm_g[None, :],
        w_ssd_out=w_ssd_out.astype(BF16), w_sb_out=w_sb_out.astype(BF16), w_out=w_out.astype(BF16),
        ln1_g=ln1_g[None, :], ln1_b=ln1_b[None, :],
        w_gu=w_gu.astype(BF16), w_down=w_down.astype(BF16),
        w_ple_gate=w_ple_gate.astype(BF16), w_ple_proj=w_ple_proj.astype(BF16),
        ln2_g=ln2_g[None, :], ln2_b=ln2_b[None, :],
    )


def _tail(x2d, pe2d, y_ssd2d, o_sb2d, w, tm):
    h = _merge_ln(x2d, y_ssd2d, o_sb2d, w, tm)
    return _ffn_ln(h, pe2d, w, tm)


def _prompt_group(x, pe, w, sb_bias, tm=256):
    bt, s, _ = x.shape
    x2d = x.reshape(bt * s, D_MODEL)
    y_ssd, conv8, ssm = _ssd_branch(x, jnp.zeros((bt, SUBLANES, CONV_DIM), F32),
                                    jnp.zeros((bt, D_INNER, D_STATE), F32), w, CHUNK)
    q, k, v, kb, vb = _qkv_proj(x2d, w["w_qkv"], tm)
    shp = (bt, s, SB_WIDTH)
    o_sb = _sb_prompt(q.reshape(shp), kb.reshape(shp), vb.reshape(shp), sb_bias)
    out = _tail(x2d, pe.reshape(bt * s, PLE_DIM), y_ssd.reshape(bt * s, D_INNER),
                o_sb.reshape(bt * s, SB_WIDTH), w, tm)
    return (out.reshape(bt, s, D_MODEL), conv8[:, SUBLANES - (CONV_W - 1):],
            ssm.reshape(bt, SSD_HEADS, SSD_HEAD_DIM, D_STATE),
            k.reshape(bt, s, SB_HEADS, SB_HEAD_DIM), v.reshape(bt, s, SB_HEADS, SB_HEAD_DIM))


def _sample_group(x, pe, state_conv, state_ssm, cache_k, cache_v, page_table, w, sb_bias):
    bt, n_tok, _ = x.shape
    t = bt * n_tok
    x2d = x.reshape(t, D_MODEL)
    x_pad = jnp.pad(x, ((0, 0), (0, CHUNK - n_tok), (0, 0)))
    conv8 = jnp.pad(state_conv, ((0, 0), (SUBLANES - (CONV_W - 1), 0), (0, 0)))
    y_pad, conv_new, ssm = _ssd_branch(x_pad, conv8, state_ssm.reshape(bt, D_INNER, D_STATE), w, n_tok)
    y_ssd = y_pad[:, :n_tok].reshape(t, D_INNER)
    q, k, v, kb, vb = _qkv_proj(x2d, w["w_qkv"], t)
    shp = (bt, n_tok, SB_WIDTH)
    n_pool = cache_k.shape[0]
    o_sb = _sb_sample(q.reshape(shp), kb.reshape(shp), vb.reshape(shp),
                      cache_k.reshape(n_pool, PAGE, SB_WIDTH), cache_v.reshape(n_pool, PAGE, SB_WIDTH),
                      page_table, sb_bias)
    out = _tail(x2d, pe.reshape(t, PLE_DIM), y_ssd, o_sb.reshape(t, SB_WIDTH), w, t)
    return (out.reshape(bt, n_tok, D_MODEL), conv_new[:, SUBLANES - (CONV_W - 1):],
            ssm.reshape(bt, SSD_HEADS, SSD_HEAD_DIM, D_STATE),
            k.reshape(bt, n_tok, SB_HEADS, SB_HEAD_DIM), v.reshape(bt, n_tok, SB_HEADS, SB_HEAD_DIM))


def kernel(x_prompt, x_sample, cache_k, cache_v, state_conv, state_ssm, page_table, p_prompt, p_sample,
           w_in, conv_w, conv_b, dt_bias, a_log, d_skip, ssd_norm_g, sb_bias, w_ssd_out, w_sb_out, w_out,
           ln1_g, ln1_b, w_gu, w_down, w_ple_gate, w_ple_proj, ln2_g, ln2_b):
    assert w_in.shape[0] == DEPTH
    w = _prep_weights(w_in[0], conv_w[0], conv_b[0], dt_bias[0], a_log[0], d_skip[0], ssd_norm_g[0],
                      w_ssd_out[0], w_sb_out[0], w_out[0], ln1_g[0], ln1_b[0], w_gu[0], w_down[0],
                      w_ple_gate[0], w_ple_proj[0], ln2_g[0], ln2_b[0])
    bias = sb_bias[0]
    yp, cp, sp, kp, vp = _prompt_group(x_prompt, p_prompt[0], w, bias)
    ys, cs, ss, ks, vs = _sample_group(x_sample, p_sample[0], state_conv[0], state_ssm[0],
                                       cache_k[0], cache_v[0], page_table, w, bias)
    return (yp, ys, kp[None], vp[None], cp[None], sp[None], ks[None], vs[None], cs[None], ss[None])
```

```python
import functools

import jax
import jax.numpy as jnp
from jax import lax
from jax.experimental import pallas as pl
from jax.experimental.pallas import tpu as pltpu

F32 = jnp.float32
BF16 = jnp.bfloat16

D_MODEL = 1024
D_INNER = 2048
SSD_HEADS = 32
SSD_GROUPS = 8
SSD_HPG = SSD_HEADS // SSD_GROUPS
SSD_HEAD_DIM = 64
D_STATE = 128
CONV_W = 4
CONV_DIM = D_INNER + 2 * SSD_GROUPS * D_STATE
CHUNK = 128
SB_HEADS = 16
SB_HEAD_DIM = 64
SB_WIDTH = SB_HEADS * SB_HEAD_DIM
SB_SCALE = SB_HEAD_DIM ** -0.5
PAGE = 128
SB_TILE = 256
SB_Q_SUBBLOCKS = 2
MAX_PAGES_PER_STEP = 8
LOG2E = 1.4426950408889634
D_FF = 2816
PLE_DIM = 256
LN_EPS = 1e-5
RMS_EPS = 1e-5
DEPTH = 1
DN_ALPHA = (2 * DEPTH) ** 0.25

LANES = 128
SUBLANES = 8
VMEM_LIMIT = 56 * 1024 * 1024

_IN_SIZES = (D_INNER, CONV_DIM, SSD_HEADS, SB_WIDTH, SB_WIDTH, SB_WIDTH, D_MODEL, D_MODEL)


def _dot(a, b):
    return jnp.dot(a, b, preferred_element_type=F32)


def _dot_nt(a, b):
    return lax.dot_general(a, b, (((1,), (1,)), ((), ())), preferred_element_type=F32)


def _softplus(x):
    return jnp.maximum(x, 0.0) + jnp.log1p(jnp.exp(-jnp.abs(x)))


def _silu(x):
    return x * jax.nn.sigmoid(x)


def _split3(x):
    hi = x.astype(BF16)
    r = x - hi.astype(F32)
    mid = r.astype(BF16)
    lo = (r - mid.astype(F32)).astype(BF16)
    return hi, mid, lo


def _layer_norm(x, g, b):
    mu = jnp.mean(x, axis=-1, keepdims=True)
    d = x - mu
    var = jnp.mean(d * d, axis=-1, keepdims=True)
    return d * lax.rsqrt(var + LN_EPS) * g + b


def _full(shape):
    n = len(shape)
    return pl.BlockSpec(shape, lambda *_: (0,) * n)


def _ssd_kernel(x_ref, cs_ref, s0_ref, wz_ref, wx_ref, wdt_ref, cw_ref, cb_ref, dtb_ref, alog_ref,
                dsk_ref, g_ref, y_ref, co_ref, so_ref, xpad, act, *, n_valid_last):
    i = pl.program_id(1)
    n_i = pl.num_programs(1)
    L = CHUNK

    @pl.when(i == 0)
    def _():
        xpad[0:SUBLANES, :] = cs_ref[...]
        so_ref[...] = s0_ref[...]

    xb = x_ref[...].astype(BF16)
    xpad[SUBLANES:SUBLANES + L, :] = _dot(xb, wx_ref[...])

    for c in range(0, CONV_DIM, 512):
        sl = slice(c, c + 512)
        conv = cb_ref[:, sl]
        for j in range(CONV_W):
            r0 = SUBLANES - (CONV_W - 1) + j
            conv = conv + cw_ref[j:j + 1, sl] * xpad[r0:r0 + L, sl]
        act[:, sl] = _silu(conv)

    @pl.when(i == n_i - 1)
    def _():
        co_ref[...] = xpad[n_valid_last:n_valid_last + SUBLANES, :]

    xpad[0:SUBLANES, :] = xpad[L:L + SUBLANES, :]

    row = lax.broadcasted_iota(jnp.int32, (L, LANES), 0)
    col = lax.broadcasted_iota(jnp.int32, (L, LANES), 1)
    dt = _softplus(_dot(xb, wdt_ref[...]) + dtb_ref[...])
    limit = jnp.where(i == n_i - 1, n_valid_last, L)
    dt = jnp.where(row < limit, dt, 0.0)
    d_a = dt * (-jnp.exp(alog_ref[...]))
    causal = row >= col
    tril = causal.astype(BF16)
    hi, mid, lo = _split3(d_a)
    acum = _dot(tril, hi) + _dot(tril, mid) + _dot(tril, lo)
    acum_t = acum.T
    e_acum = jnp.exp(acum)
    w_in = jnp.exp(acum[L - 1:L, :] - acum) * dt
    e_last_t = jnp.exp(acum_t[:, L - 1:L])
    lo_half = col < SSD_HEAD_DIM
    gw = SSD_HPG * SSD_HEAD_DIM

    for g in range(SSD_GROUPS):
        b_g = act[:, D_INNER + D_STATE * g:D_INNER + D_STATE * (g + 1)].astype(BF16)
        c0 = D_INNER + SSD_GROUPS * D_STATE + D_STATE * g
        c_g = act[:, c0:c0 + D_STATE].astype(BF16)
        cb = _dot_nt(c_g, b_g)
        h_g = so_ref[gw * g:gw * (g + 1), :]
        inter = _dot_nt(c_g, h_g.astype(BF16))
        z_g = _dot(xb, wz_ref[:, gw * g:gw * (g + 1)])
        us, xws = [], []
        for jj in range(2):
            h0 = SSD_HPG * g + 2 * jj
            h1 = h0 + 1
            x0 = gw * g + LANES * jj
            x2 = act[:, x0:x0 + LANES]
            ms = []
            for h in (h0, h1):
                seg = acum[:, h:h + 1] - acum_t[h:h + 1, :]
                dec = jnp.exp(jnp.where(causal, seg, -jnp.inf))
                ms.append((cb * dec).astype(BF16))
            m_cat = jnp.concatenate(ms, axis=1)
            xdt = x2 * jnp.where(lo_half, dt[:, h0:h0 + 1], dt[:, h1:h1 + 1])
            x_stack = jnp.concatenate([jnp.where(lo_half, xdt, 0.0).astype(BF16),
                                       jnp.where(lo_half, 0.0, xdt).astype(BF16)], axis=0)
            y2 = _dot(m_cat, x_stack)
            e2 = jnp.where(lo_half, e_acum[:, h0:h0 + 1], e_acum[:, h1:h1 + 1])
            y2 = y2 + inter[:, LANES * jj:LANES * (jj + 1)] * e2 + dsk_ref[:, x0:x0 + LANES] * x2
            us.append(y2 * _silu(z_g[:, LANES * jj:LANES * (jj + 1)]))
            xws.append(x2 * jnp.where(lo_half, w_in[:, h0:h0 + 1], w_in[:, h1:h1 + 1]))
        ss = (jnp.sum(us[0] * us[0], axis=-1, keepdims=True)
              + jnp.sum(us[1] * us[1], axis=-1, keepdims=True))
        scale = lax.rsqrt(ss * (1.0 / gw) + RMS_EPS)
        for jj in range(2):
            x0 = gw * g + LANES * jj
            y_ref[:, x0:x0 + LANES] = (us[jj] * scale * g_ref[:, x0:x0 + LANES]).astype(y_ref.dtype)
        xw_t = jnp.concatenate(xws, axis=1).T
        e_rows = jnp.concatenate(
            [jnp.broadcast_to(e_last_t[SSD_HPG * g + r:SSD_HPG * g + r + 1, :], (SSD_HEAD_DIM, D_STATE))
             for r in range(SSD_HPG)], axis=0)
        so_ref[gw * g:gw * (g + 1), :] = h_g * e_rows + _dot(xw_t.astype(BF16), b_g)


def _ssd_branch(x, conv_state8, ssm0, w, n_valid_last):
    bt, t, _ = x.shape
    n_c = t // CHUNK
    kern = functools.partial(_ssd_kernel, n_valid_last=n_valid_last)
    return pl.pallas_call(
        kern,
        grid=(bt, n_c),
        in_specs=[
            pl.BlockSpec((None, CHUNK, D_MODEL), lambda b, i: (b, i, 0)),
            pl.BlockSpec((None, SUBLANES, CONV_DIM), lambda b, i: (b, 0, 0)),
            pl.BlockSpec((None, D_INNER, D_STATE), lambda b, i: (b, 0, 0)),
            _full((D_MODEL, D_INNER)),
            _full((D_MODEL, CONV_DIM)),
            _full((D_MODEL, LANES)),
            _full((CONV_W, CONV_DIM)),
            _full((1, CONV_DIM)),
            _full((1, LANES)),
            _full((1, LANES)),
            _full((1, D_INNER)),
            _full((1, D_INNER)),
        ],
        out_specs=[
            pl.BlockSpec((None, CHUNK, D_INNER), lambda b, i: (b, i, 0)),
            pl.BlockSpec((None, SUBLANES, CONV_DIM), lambda b, i: (b, 0, 0)),
            pl.BlockSpec((None, D_INNER, D_STATE), lambda b, i: (b, 0, 0)),
        ],
        out_shape=[
            jax.ShapeDtypeStruct((bt, t, D_INNER), BF16),
            jax.ShapeDtypeStruct((bt, SUBLANES, CONV_DIM), F32),
            jax.ShapeDtypeStruct((bt, D_INNER, D_STATE), F32),
        ],
        scratch_shapes=[
            pltpu.VMEM((SUBLANES + CHUNK, CONV_DIM), F32),
            pltpu.VMEM((CHUNK, CONV_DIM), F32),
        ],
        compiler_params=pltpu.CompilerParams(
            dimension_semantics=("parallel", "arbitrary"), vmem_limit_bytes=VMEM_LIMIT),
        name="ssd_branch",
    )(x, conv_state8, ssm0, w["w_z"], w["w_xbc"], w["w_dt"], w["conv_w"], w["conv_b"], w["dt_bias"],
      w["a_log"], w["d_skip"], w["ssd_norm_g"])


def _qkv_kernel(x_ref, w_ref, q_ref, k_ref, v_ref, kb_ref, vb_ref):
    xb = x_ref[...].astype(BF16)
    q_ref[...] = (_dot(xb, w_ref[:, 0:SB_WIDTH]) * (SB_SCALE * LOG2E)).astype(BF16)
    k = _dot(xb, w_ref[:, SB_WIDTH:2 * SB_WIDTH])
    k_ref[...] = k
    kb_ref[...] = k.astype(BF16)
    v = _dot(xb, w_ref[:, 2 * SB_WIDTH:3 * SB_WIDTH])
    v_ref[...] = v
    vb_ref[...] = v.astype(BF16)


def _qkv_proj(x2d, w_qkv, tm):
    t = x2d.shape[0]
    row = lambda i: (i, 0)
    return pl.pallas_call(
        _qkv_kernel,
        grid=(t // tm,),
        in_specs=[pl.BlockSpec((tm, D_MODEL), row), _full((D_MODEL, 3 * SB_WIDTH))],
        out_specs=[pl.BlockSpec((tm, SB_WIDTH), row)] * 5,
        out_shape=[
            jax.ShapeDtypeStruct((t, SB_WIDTH), BF16),
            jax.ShapeDtypeStruct((t, SB_WIDTH), F32),
            jax.ShapeDtypeStruct((t, SB_WIDTH), F32),
            jax.ShapeDtypeStruct((t, SB_WIDTH), BF16),
            jax.ShapeDtypeStruct((t, SB_WIDTH), BF16),
        ],
        compiler_params=pltpu.CompilerParams(
            dimension_semantics=("parallel",), vmem_limit_bytes=VMEM_LIMIT),
        name="qkv_proj",
    )(x2d, w_qkv)


def _qkv_prompt_kernel(x_ref, w_ref, q_ref, kt_ref, vt_ref, ktb_ref, vst_ref):
    xb = x_ref[...].astype(BF16)
    t = xb.shape[0]
    q_ref[...] = (_dot(xb, w_ref[:, 0:SB_WIDTH]) * (SB_SCALE * LOG2E)).astype(BF16)
    k_t = _dot(xb, w_ref[:, SB_WIDTH:2 * SB_WIDTH]).T
    kt_ref[...] = k_t
    ktb_ref[...] = k_t.astype(BF16)
    v = _dot(xb, w_ref[:, 2 * SB_WIDTH:3 * SB_WIDTH])
    vt_ref[...] = v.T
    lo_half = lax.broadcasted_iota(jnp.int32, (t, LANES), 1) < SB_HEAD_DIM
    for hp in range(SB_WIDTH // LANES):
        v2 = v[:, LANES * hp:LANES * (hp + 1)]
        vst_ref[hp, 0:t, :] = jnp.where(lo_half, v2, 0.0).astype(BF16)
        vst_ref[hp, t:2 * t, :] = jnp.where(lo_half, 0.0, v2).astype(BF16)


def _qkv_prompt(x, w_qkv):
    b, s, _ = x.shape
    t = SB_TILE
    n_hp = SB_WIDTH // LANES
    return pl.pallas_call(
        _qkv_prompt_kernel,
        grid=(b, s // t),
        in_specs=[pl.BlockSpec((None, t, D_MODEL), lambda bi, i: (bi, i, 0)),
                  _full((D_MODEL, 3 * SB_WIDTH))],
        out_specs=[
            pl.BlockSpec((None, t, SB_WIDTH), lambda bi, i: (bi, i, 0)),
            pl.BlockSpec((None, SB_WIDTH, t), lambda bi, i: (bi, 0, i)),
            pl.BlockSpec((None, SB_WIDTH, t), lambda bi, i: (bi, 0, i)),
            pl.BlockSpec((None, SB_WIDTH, t), lambda bi, i: (bi, 0, i)),
            pl.BlockSpec((None, None, n_hp, 2 * t, LANES), lambda bi, i: (bi, i, 0, 0, 0)),
        ],
        out_shape=[
            jax.ShapeDtypeStruct((b, s, SB_WIDTH), BF16),
            jax.ShapeDtypeStruct((b, SB_WIDTH, s), F32),
            jax.ShapeDtypeStruct((b, SB_WIDTH, s), F32),
            jax.ShapeDtypeStruct((b, SB_WIDTH, s), BF16),
            jax.ShapeDtypeStruct((b, s // t, n_hp, 2 * t, LANES), BF16),
        ],
        compiler_params=pltpu.CompilerParams(
            dimension_semantics=("parallel", "parallel"), vmem_limit_bytes=VMEM_LIMIT),
        name="qkv_prompt",
    )(x, w_qkv)


def _suffix_matrix(t):
    r = lax.broadcasted_iota(jnp.int32, (t, t), 0)
    c = lax.broadcasted_iota(jnp.int32, (t, t), 1)
    return (r > c).astype(BF16)


def _neg_abs(x):
    bits = lax.bitcast_convert_type(x, jnp.uint32) | jnp.uint32(0x80000000)
    return lax.bitcast_convert_type(bits, F32)


def _sp2(z2, valid):
    sp = jnp.maximum(z2, 0.0) + jnp.log(1.0 + jnp.exp2(_neg_abs(z2))) * LOG2E
    return sp if valid is None else jnp.where(valid, sp, 0.0)


def _sb_tile(z2, u, valid):
    sp = _sp2(z2, valid)
    suffix = _dot(sp.astype(BF16), u)
    a = jnp.exp2(z2 - sp - suffix)
    if valid is not None:
        a = jnp.where(valid, a, 0.0)
    return a.astype(BF16), jnp.sum(sp, axis=-1, keepdims=True)


def _sbp_kernel(bias_ref, q_ref, k_ref, v_ref, u_ref, o_ref):
    hp = pl.program_id(1)
    qi = pl.program_id(2)
    t = SB_TILE
    n_sub = q_ref.shape[0] // t
    b0 = bias_ref[2 * hp] * LOG2E
    b1 = bias_ref[2 * hp + 1] * LOG2E
    lo_half = lax.broadcasted_iota(jnp.int32, (t, LANES), 1) < SB_HEAD_DIM
    qs = []
    for s in range(n_sub):
        q2 = q_ref[s * t:(s + 1) * t, :].astype(F32)
        qs.append(jnp.concatenate([jnp.where(lo_half, q2, 0.0), jnp.where(lo_half, 0.0, q2)],
                                  axis=0).astype(BF16))
    u = u_ref[...]
    r = lax.broadcasted_iota(jnp.int32, (t, t), 0)
    c = lax.broadcasted_iota(jnp.int32, (t, t), 1)
    diag_valid = c < r

    def chain(z, ls_pair, v_tile, valid):
        a0, rs0 = _sb_tile(z[0:t] + b0, u, valid)
        a1, rs1 = _sb_tile(z[t:2 * t] + b1, u, valid)
        out = _dot(jnp.concatenate([a0, a1], axis=1), v_tile)
        w = jnp.where(lo_half, jnp.exp2(-ls_pair[0]), jnp.exp2(-ls_pair[1]))
        return out * w, (ls_pair[0] + rs0, ls_pair[1] + rs1)

    def tile(kt, carry, valids):
        accs, lss = list(carry[0]), list(carry[1])
        k0 = pl.multiple_of(kt * t, t)
        v_tile = v_ref[kt]
        live = [s for s in range(n_sub) if not isinstance(valids[s], str)]
        z_all = _dot(jnp.concatenate([qs[s] for s in live], axis=0), k_ref[:, pl.ds(k0, t)])
        for i, s in enumerate(live):
            d, lss[s] = chain(z_all[2 * t * i:2 * t * (i + 1)], lss[s], v_tile, valids[s])
            accs[s] = accs[s] + d
        return tuple(accs), tuple(lss)

    zero_col = jnp.zeros((t, 1), F32)
    carry = (tuple(jnp.zeros((t, LANES), F32) for _ in range(n_sub)),
             tuple((zero_col, zero_col) for _ in range(n_sub)))
    for d in range(n_sub - 1, -1, -1):
        valids = ["skip" if s < d else (diag_valid if s == d else None) for s in range(n_sub)]
        carry = tile(n_sub * qi + d, carry, valids)
    n_full = n_sub * qi
    accs, _ = lax.fori_loop(0, n_full, lambda i, cr: tile(n_full - 1 - i, cr, [None] * n_sub), carry)
    for s in range(n_sub):
        o_ref[s * t:(s + 1) * t, :] = accs[s].astype(o_ref.dtype)


def _sb_prompt(q, ktb, vst, sb_bias):
    b, s, _ = q.shape
    t = SB_TILE
    tq = SB_Q_SUBBLOCKS * t
    n_hp = SB_WIDTH // LANES
    grid_spec = pltpu.PrefetchScalarGridSpec(
        num_scalar_prefetch=1,
        grid=(b, n_hp, s // tq),
        in_specs=[
            pl.BlockSpec((None, tq, LANES), lambda bi, hp, qi, bias: (bi, qi, hp)),
            pl.BlockSpec((None, LANES, s), lambda bi, hp, qi, bias: (bi, hp, 0)),
            pl.BlockSpec((None, s // t, None, 2 * t, LANES), lambda bi, hp, qi, bias: (bi, 0, hp, 0, 0)),
            pl.BlockSpec((t, t), lambda bi, hp, qi, bias: (0, 0)),
        ],
        out_specs=pl.BlockSpec((None, tq, LANES), lambda bi, hp, qi, bias: (bi, qi, hp)),
    )
    return pl.pallas_call(
        _sbp_kernel,
        grid_spec=grid_spec,
        out_shape=jax.ShapeDtypeStruct((b, s, SB_WIDTH), BF16),
        compiler_params=pltpu.CompilerParams(
            dimension_semantics=("parallel", "parallel", "arbitrary"), vmem_limit_bytes=VMEM_LIMIT),
        name="sb_prompt",
    )(sb_bias, q, ktb, vst, _suffix_matrix(t))


def _sbs_kernel(pt_ref, q_ref, kn_ref, vn_ref, bias_ref, u_ref, *rest, n_tok, n_pp):
    kp_refs, vp_refs = rest[0:n_pp], rest[n_pp:2 * n_pp]
    o_ref, acc, ls = rest[2 * n_pp:]
    j = pl.program_id(1)
    n_j = pl.num_programs(1)
    n_rows = q_ref.shape[0]
    row = lax.broadcasted_iota(jnp.int32, (n_rows, SB_WIDTH), 0)
    lane = lax.broadcasted_iota(jnp.int32, (n_rows, SB_WIDTH), 1)
    head_mask = (row & (SB_HEADS - 1)) == (lane >> 6)
    q_bd = jnp.where(head_mask, q_ref[...].astype(F32), 0.0).astype(BF16)
    bias2 = bias_ref[...] * LOG2E
    u2 = u_ref[...]

    @pl.when(j == 0)
    def _():
        tok = lax.broadcasted_iota(jnp.int32, (n_rows, PAGE), 0) >> 4
        key = lax.broadcasted_iota(jnp.int32, (n_rows, PAGE), 1)
        a, rs = _sb_tile(_dot_nt(q_bd, kn_ref[...]) + bias2, u2[0:PAGE, 0:PAGE], key < tok)
        acc[...] = _dot(a, vn_ref[...])
        ls[...] = rs

    k_cat = jnp.concatenate([r[...].astype(BF16) for r in kp_refs], axis=1)
    z = _dot(q_bd, k_cat) + jnp.concatenate([bias2] * n_pp, axis=1)
    sp = _sp2(z, None)
    off = ls[...]
    a_parts = []
    for pr in range(n_pp // 2):
        sl = slice(2 * PAGE * pr, 2 * PAGE * (pr + 1))
        zs = z[:, sl] - sp[:, sl] - _dot(sp[:, sl].astype(BF16), u2)
        for p in range(2):
            ps = slice(PAGE * p, PAGE * (p + 1))
            a_parts.append(jnp.exp2(zs[:, ps] - off).astype(BF16))
            off = off + jnp.sum(sp[:, sl][:, ps], axis=-1, keepdims=True)
    v_cat = jnp.concatenate([r[...].astype(BF16) for r in vp_refs], axis=1)
    acc[...] += _dot_nt(jnp.concatenate(a_parts, axis=1), v_cat)
    ls[...] = off

    @pl.when(j == n_j - 1)
    def _():
        own = jnp.where(head_mask, acc[...], 0.0)
        o_ref[...] = own.reshape(n_tok, SB_HEADS, SB_WIDTH).sum(axis=1)


def _pages_per_step(n_pages):
    for p in range(MAX_PAGES_PER_STEP, 0, -2):
        if n_pages % p == 0:
            return p
    raise ValueError("the page count must be even")


def _sb_sample(q, k_new, v_new, cache_kt, cache_vt, page_table, sb_bias):
    bt, n_tok, _ = q.shape
    n_pages = page_table.shape[1]
    n_pp = _pages_per_step(n_pages)
    n_rows = n_tok * SB_HEADS
    q_rep = jnp.repeat(q, SB_HEADS, axis=1)
    pad = ((0, 0), (0, PAGE - n_tok), (0, 0))
    kn = jnp.pad(k_new, pad)
    vn = jnp.pad(v_new, pad)
    bias_rows = jnp.broadcast_to(jnp.tile(sb_bias, n_tok)[:, None], (n_rows, PAGE)).astype(F32)
    u1 = _suffix_matrix(PAGE)
    zero = jnp.zeros_like(u1)
    u2 = jnp.concatenate([jnp.concatenate([u1, zero], axis=1), jnp.concatenate([zero, u1], axis=1)], axis=0)
    per_b = lambda b, j, pt: (b, 0, 0)
    const = lambda b, j, pt: (0, 0)

    def page(p):
        return lambda b, j, pt: (pt[b, n_pages - 1 - (j * n_pp + p)], 0, 0)

    page_specs = [pl.BlockSpec((None, SB_WIDTH, PAGE), page(p)) for p in range(n_pp)]
    grid_spec = pltpu.PrefetchScalarGridSpec(
        num_scalar_prefetch=1,
        grid=(bt, n_pages // n_pp),
        in_specs=[
            pl.BlockSpec((None, n_rows, SB_WIDTH), per_b),
            pl.BlockSpec((None, PAGE, SB_WIDTH), per_b),
            pl.BlockSpec((None, PAGE, SB_WIDTH), per_b),
            pl.BlockSpec((n_rows, PAGE), const),
            pl.BlockSpec((2 * PAGE, 2 * PAGE), const),
        ] + page_specs + page_specs,
        out_specs=pl.BlockSpec((None, n_tok, SB_WIDTH), per_b),
        scratch_shapes=[pltpu.VMEM((n_rows, SB_WIDTH), F32), pltpu.VMEM((n_rows, 1), F32)],
    )
    return pl.pallas_call(
        functools.partial(_sbs_kernel, n_tok=n_tok, n_pp=n_pp),
        grid_spec=grid_spec,
        out_shape=jax.ShapeDtypeStruct((bt, n_tok, SB_WIDTH), F32),
        compiler_params=pltpu.CompilerParams(
            dimension_semantics=("parallel", "arbitrary"), vmem_limit_bytes=VMEM_LIMIT),
        name="sb_sample",
    )(page_table, q_rep, kn, vn, bias_rows, u2, *([cache_kt] * n_pp), *([cache_vt] * n_pp))


def _merge_kernel(x_ref, y_ref, o_ref, wg_ref, wso_ref, wbo_ref, wo_ref, g_ref, b_ref, h_ref):
    x = x_ref[...]
    xb = x.astype(BF16)
    ga = _dot(xb, wg_ref[:, 0:D_MODEL])
    gb = _dot(xb, wg_ref[:, D_MODEL:2 * D_MODEL])
    p_ssd = _dot(y_ref[...].astype(BF16), wso_ref[...])
    p_sb = _dot(o_ref[...].astype(BF16), wbo_ref[...])
    merged = jax.nn.sigmoid(ga) * p_ssd + jax.nn.sigmoid(gb) * p_sb
    pre = DN_ALPHA * x + _dot(merged.astype(BF16), wo_ref[...])
    h_ref[...] = _layer_norm(pre, g_ref[...], b_ref[...])


def _merge_ln(x2d, y_ssd, o_sb, w, tm):
    t = x2d.shape[0]
    row = lambda i: (i, 0)
    return pl.pallas_call(
        _merge_kernel,
        grid=(t // tm,),
        in_specs=[
            pl.BlockSpec((tm, D_MODEL), row),
            pl.BlockSpec((tm, D_INNER), row),
            pl.BlockSpec((tm, SB_WIDTH), row),
            _full((D_MODEL, 2 * D_MODEL)),
            _full((D_INNER, D_MODEL)),
            _full((SB_WIDTH, D_MODEL)),
            _full((D_MODEL, D_MODEL)),
            _full((1, D_MODEL)),
            _full((1, D_MODEL)),
        ],
        out_specs=pl.BlockSpec((tm, D_MODEL), row),
        out_shape=jax.ShapeDtypeStruct((t, D_MODEL), F32),
        compiler_params=pltpu.CompilerParams(
            dimension_semantics=("parallel",), vmem_limit_bytes=VMEM_LIMIT),
        name="merge_ln",
    )(x2d, y_ssd, o_sb, w["w_gate"], w["w_ssd_out"], w["w_sb_out"], w["w_out"], w["ln1_g"], w["ln1_b"])


def _ffn_kernel(h_ref, pe_ref, wgu_ref, wd_ref, wpg_ref, wpp_ref, g_ref, b_ref, o_ref):
    h = h_ref[...]
    hb = h.astype(BF16)
    g_ = _dot(hb, wgu_ref[:, 0:D_FF])
    u_ = _dot(hb, wgu_ref[:, D_FF:2 * D_FF])
    ffn = _dot((_silu(g_) * u_).astype(BF16), wd_ref[...])
    ple = jax.nn.sigmoid(_dot(hb, wpg_ref[...])) * _dot(pe_ref[...].astype(BF16), wpp_ref[...])
    o_ref[...] = _layer_norm(DN_ALPHA * h + ffn + ple, g_ref[...], b_ref[...])


def _ffn_ln(h2d, pe2d, w, tm):
    t = h2d.shape[0]
    row = lambda i: (i, 0)
    once = pl.Buffered(1)
    return pl.pallas_call(
        _ffn_kernel,
        grid=(t // tm,),
        in_specs=[
            pl.BlockSpec((tm, D_MODEL), row),
            pl.BlockSpec((tm, PLE_DIM), row),
            pl.BlockSpec((D_MODEL, 2 * D_FF), lambda i: (0, 0), pipeline_mode=once),
            pl.BlockSpec((D_FF, D_MODEL), lambda i: (0, 0), pipeline_mode=once),
            pl.BlockSpec((D_MODEL, D_MODEL), lambda i: (0, 0), pipeline_mode=once),
            pl.BlockSpec((PLE_DIM, D_MODEL), lambda i: (0, 0), pipeline_mode=once),
            _full((1, D_MODEL)),
            _full((1, D_MODEL)),
        ],
        out_specs=pl.BlockSpec((tm, D_MODEL), row),
        out_shape=jax.ShapeDtypeStruct((t, D_MODEL), F32),
        compiler_params=pltpu.CompilerParams(
            dimension_semantics=("parallel",), vmem_limit_bytes=VMEM_LIMIT),
        name="ffn_ln",
    )(h2d, pe2d, w["w_gu"], w["w_down"], w["w_ple_gate"], w["w_ple_proj"], w["ln2_g"], w["ln2_b"])


def _prep_weights(w_in, conv_w, conv_b, dt_bias, a_log, d_skip, ssd_norm_g, w_ssd_out, w_sb_out, w_out,
                  ln1_g, ln1_b, w_gu, w_down, w_ple_gate, w_ple_proj, ln2_g, ln2_b):
    offs = [0]
    for s in _IN_SIZES:
        offs.append(offs[-1] + s)
    seg = lambda k: w_in[:, offs[k]:offs[k + 1]]
    lane_pad = lambda a: jnp.pad(a, ((0, 0), (0, LANES - a.shape[1])))
    return dict(
        w_z=seg(0).astype(BF16),
        w_xbc=seg(1).astype(BF16),
        w_dt=lane_pad(seg(2)).astype(BF16),
        w_qkv=jnp.concatenate([seg(3), seg(4), seg(5)], axis=1).astype(BF16),
        w_gate=jnp.concatenate([seg(6), seg(7)], axis=1).astype(BF16),
        conv_w=conv_w, conv_b=conv_b[None, :],
        dt_bias=lane_pad(dt_bias[None, :]), a_log=lane_pad(a_log[None, :]),
        d_skip=jnp.repeat(d_skip, SSD_HEAD_DIM)[None, :], ssd_norm_g=ssd_norm_g[None, :],
        w_ssd_out=w_ssd_out.astype(BF16), w_sb_out=w_sb_out.astype(BF16), w_out=w_out.astype(BF16),
        ln1_g=ln1_g[None, :], ln1_b=ln1_b[None, :],
        w_gu=w_gu.astype(BF16), w_down=w_down.astype(BF16),
        w_ple_gate=w_ple_gate.astype(BF16), w_ple_proj=w_ple_proj.astype(BF16),
        ln2_g=ln2_g[None, :], ln2_b=ln2_b[None, :],
    )


def _tail(x2d, pe2d, y_ssd2d, o_sb2d, w, tm):
    h = _merge_ln(x2d, y_ssd2d, o_sb2d, w, tm)
    return _ffn_ln(h, pe2d, w, tm)


def _heads_last(a_t, bt, s):
    return jnp.transpose(a_t.reshape(bt, SB_HEADS, SB_HEAD_DIM, s), (0, 3, 1, 2))


def _prompt_group(x, pe, w, sb_bias, tm=256):
    bt, s, _ = x.shape
    x2d = x.reshape(bt * s, D_MODEL)
    y_ssd, conv8, ssm = _ssd_branch(x, jnp.zeros((bt, SUBLANES, CONV_DIM), F32),
                                    jnp.zeros((bt, D_INNER, D_STATE), F32), w, CHUNK)
    q, k_t, v_t, ktb, vst = _qkv_prompt(x, w["w_qkv"])
    o_sb = _sb_prompt(q, ktb, vst, sb_bias)
    out = _tail(x2d, pe.reshape(bt * s, PLE_DIM), y_ssd.reshape(bt * s, D_INNER),
                o_sb.reshape(bt * s, SB_WIDTH), w, tm)
    return (out.reshape(bt, s, D_MODEL), conv8[:, SUBLANES - (CONV_W - 1):],
            ssm.reshape(bt, SSD_HEADS, SSD_HEAD_DIM, D_STATE),
            _heads_last(k_t, bt, s), _heads_last(v_t, bt, s))


def _sample_group(x, pe, state_conv, state_ssm, cache_k, cache_v, page_table, w, sb_bias):
    bt, n_tok, _ = x.shape
    t = bt * n_tok
    x2d = x.reshape(t, D_MODEL)
    x_pad = jnp.pad(x, ((0, 0), (0, CHUNK - n_tok), (0, 0)))
    conv8 = jnp.pad(state_conv, ((0, 0), (SUBLANES - (CONV_W - 1), 0), (0, 0)))
    y_pad, conv_new, ssm = _ssd_branch(x_pad, conv8, state_ssm.reshape(bt, D_INNER, D_STATE), w, n_tok)
    y_ssd = y_pad[:, :n_tok].reshape(t, D_INNER)
    q, k, v, kb, vb = _qkv_proj(x2d, w["w_qkv"], t)
    shp = (bt, n_tok, SB_WIDTH)
    n_pool = cache_k.shape[0]
    cache_kt = jnp.transpose(cache_k, (0, 2, 3, 1)).reshape(n_pool, SB_WIDTH, PAGE)
    cache_vt = jnp.transpose(cache_v, (0, 2, 3, 1)).reshape(n_pool, SB_WIDTH, PAGE)
    o_sb = _sb_sample(q.reshape(shp), kb.reshape(shp), vb.reshape(shp), cache_kt, cache_vt, page_table, sb_bias)
    out = _tail(x2d, pe.reshape(t, PLE_DIM), y_ssd, o_sb.reshape(t, SB_WIDTH), w, t)
    return (out.reshape(bt, n_tok, D_MODEL), conv_new[:, SUBLANES - (CONV_W - 1):],
            ssm.reshape(bt, SSD_HEADS, SSD_HEAD_DIM, D_STATE),
            k.reshape(bt, n_tok, SB_HEADS, SB_HEAD_DIM), v.reshape(bt, n_tok, SB_HEADS, SB_HEAD_DIM))


def kernel(x_prompt, x_sample, cache_k, cache_v, state_conv, state_ssm, page_table, p_prompt, p_sample,
           w_in, conv_w, conv_b, dt_bias, a_log, d_skip, ssd_norm_g, sb_bias, w_ssd_out, w_sb_out, w_out,
           ln1_g, ln1_b, w_gu, w_down, w_ple_gate, w_ple_proj, ln2_g, ln2_b):
    assert w_in.shape[0] == DEPTH
    w = _prep_weights(w_in[0], conv_w[0], conv_b[0], dt_bias[0], a_log[0], d_skip[0], ssd_norm_g[0],
                      w_ssd_out[0], w_sb_out[0], w_out[0], ln1_g[0], ln1_b[0], w_gu[0], w_down[0],
                      w_ple_gate[0], w_ple_proj[0], ln2_g[0], ln2_b[0])
    bias = sb_bias[0]
    yp, cp, sp, kp, vp = _prompt_group(x_prompt, p_prompt[0], w, bias)
    ys, cs, ss, ks, vs = _sample_group(x_sample, p_sample[0], state_conv[0], state_ssm[0],
                                       cache_k[0], cache_v[0], page_table, w, bias)
    return (yp, ys, kp[None], vp[None], cp[None], sp[None], ks[None], vs[None], cs[None], ss[None])
```

```python
import functools

import jax
import jax.numpy as jnp
from jax import lax
from jax.experimental import pallas as pl
from jax.experimental.pallas import tpu as pltpu

F32 = jnp.float32
BF16 = jnp.bfloat16

D_MODEL = 1024
D_INNER = 2048
SSD_HEADS = 32
SSD_GROUPS = 8
SSD_HPG = SSD_HEADS // SSD_GROUPS
SSD_HEAD_DIM = 64
D_STATE = 128
CONV_W = 4
CONV_DIM = D_INNER + 2 * SSD_GROUPS * D_STATE
CHUNK = 128
SSD_CHUNKS_PER_STEP = 2
SB_HEADS = 16
SB_HEAD_DIM = 64
SB_WIDTH = SB_HEADS * SB_HEAD_DIM
SB_SCALE = SB_HEAD_DIM ** -0.5
PAGE = 128
SB_TILE = 256
SB_Q_SUBBLOCKS = 4
MAX_PAGES_PER_STEP = 8
LOG2E = 1.4426950408889634
D_FF = 2816
PLE_DIM = 256
LN_EPS = 1e-5
RMS_EPS = 1e-5
DEPTH = 1
DN_ALPHA = (2 * DEPTH) ** 0.25

LANES = 128
SUBLANES = 8
VMEM_LIMIT = 56 * 1024 * 1024

_IN_SIZES = (D_INNER, CONV_DIM, SSD_HEADS, SB_WIDTH, SB_WIDTH, SB_WIDTH, D_MODEL, D_MODEL)


def _dot(a, b):
    return jnp.dot(a, b, preferred_element_type=F32)


def _dot_nt(a, b):
    return lax.dot_general(a, b, (((1,), (1,)), ((), ())), preferred_element_type=F32)


def _softplus(x):
    return jnp.maximum(x, 0.0) + jnp.log1p(jnp.exp(-jnp.abs(x)))


def _silu(x):
    return x * jax.nn.sigmoid(x)


def _split3(x):
    hi = x.astype(BF16)
    r = x - hi.astype(F32)
    mid = r.astype(BF16)
    lo = (r - mid.astype(F32)).astype(BF16)
    return hi, mid, lo


def _layer_norm(x, g, b):
    mu = jnp.mean(x, axis=-1, keepdims=True)
    d = x - mu
    var = jnp.mean(d * d, axis=-1, keepdims=True)
    return d * lax.rsqrt(var + LN_EPS) * g + b


def _full(shape):
    n = len(shape)
    return pl.BlockSpec(shape, lambda *_: (0,) * n)


def _ssd_kernel(x_ref, cs_ref, s0_ref, wz_ref, wx_ref, wdt_ref, cw_ref, cb_ref, dtb_ref, alog_ref,
                dsk_ref, g_ref, e64_ref, e128_ref, y_ref, co_ref, so_ref, xpad, act, ex_ref, seg_ref, *,
                n_valid_last):
    i = pl.program_id(1)
    n_i = pl.num_programs(1)
    L = CHUNK
    rows = x_ref.shape[0]

    @pl.when(i == 0)
    def _():
        xpad[0:SUBLANES, :] = cs_ref[...]
        so_ref[...] = s0_ref[...]

    xb_all = x_ref[...].astype(BF16)
    xpad[SUBLANES:SUBLANES + rows, :] = _dot(xb_all, wx_ref[...])

    for c in range(0, CONV_DIM, 512):
        sl = slice(c, c + 512)
        xp = xpad[:, sl]
        conv = cb_ref[:, sl] + cw_ref[CONV_W - 1:CONV_W, sl] * xp[SUBLANES:, :]
        for s in range(1, CONV_W):
            shifted = pltpu.roll(xp, s, 0)[SUBLANES:, :]
            conv = conv + cw_ref[CONV_W - 1 - s:CONV_W - s, sl] * shifted
        act[:, sl] = _silu(conv)

    @pl.when(i == n_i - 1)
    def _():
        last = rows - L + n_valid_last
        co_ref[...] = xpad[last:last + SUBLANES, :]

    xpad[0:SUBLANES, :] = xpad[rows:rows + SUBLANES, :]

    dt_all = _softplus(_dot(xb_all, wdt_ref[...]) + dtb_ref[...])
    for ci in range(rows // L):
        last_chunk = (i == n_i - 1) if (ci == rows // L - 1 and n_valid_last < L) else False
        _ssd_chunk(L * ci, last_chunk, n_valid_last, xb_all[L * ci:L * (ci + 1)], dt_all[L * ci:L * (ci + 1)],
                   wz_ref, alog_ref, dsk_ref, g_ref, e64_ref, e128_ref, y_ref, so_ref, act, ex_ref, seg_ref)


def _spread(x, e_ref):
    hi = x.astype(BF16)
    lo = (x - hi.astype(F32)).astype(BF16)
    return _dot(jnp.concatenate([hi, lo], axis=1), e_ref[...])


def _head_spread_matrix(width):
    h = lax.broadcasted_iota(jnp.int32, (LANES, SSD_HEADS * width), 0)
    c = lax.broadcasted_iota(jnp.int32, (LANES, SSD_HEADS * width), 1)
    e = (c // width == h).astype(BF16)
    return jnp.concatenate([e, e], axis=0)


def _ssd_chunk(r0, last_chunk, n_valid_last, xb, dt, wz_ref, alog_ref, dsk_ref, g_ref, e64_ref, e128_ref,
               y_ref, so_ref, act_ref, ex_ref, seg_ref):
    L = CHUNK
    act = act_ref.at[r0:r0 + L, :]
    y_ref = y_ref.at[r0:r0 + L, :]
    row = lax.broadcasted_iota(jnp.int32, (L, LANES), 0)
    col = lax.broadcasted_iota(jnp.int32, (L, LANES), 1)
    if last_chunk is not False:
        limit = jnp.where(last_chunk, n_valid_last, L)
        dt = jnp.where(row < limit, dt, 0.0)
    d_a = dt * (-jnp.exp(alog_ref[...]))
    causal = row >= col
    tril = causal.astype(BF16)
    hi, mid, lo = _split3(d_a)
    acum = _dot(tril, hi) + _dot(tril, mid) + _dot(tril, lo)
    acum_t = acum.T
    e_last_t = jnp.exp(acum_t[:, L - 1:L])
    per_head = jnp.concatenate([dt, jnp.exp(acum), jnp.exp(acum[L - 1:L, :] - acum) * dt], axis=0)
    ex_ref[...] = _spread(per_head, e64_ref)
    seg_ref[...] = _spread(acum, e128_ref)
    lo_half = col < SSD_HEAD_DIM
    gw = SSD_HPG * SSD_HEAD_DIM

    for g in range(SSD_GROUPS):
        b_g = act[:, D_INNER + D_STATE * g:D_INNER + D_STATE * (g + 1)].astype(BF16)
        c0 = D_INNER + SSD_GROUPS * D_STATE + D_STATE * g
        c_g = act[:, c0:c0 + D_STATE].astype(BF16)
        cb = _dot_nt(c_g, b_g)
        h_g = so_ref[gw * g:gw * (g + 1), :]
        inter = _dot_nt(c_g, h_g.astype(BF16))
        z_g = _dot(xb, wz_ref[:, gw * g:gw * (g + 1)])
        us, xws = [], []
        for jj in range(2):
            h0 = SSD_HPG * g + 2 * jj
            h1 = h0 + 1
            x0 = gw * g + LANES * jj
            x2 = act[:, x0:x0 + LANES]
            ms = []
            for h in (h0, h1):
                seg = seg_ref[:, D_STATE * h:D_STATE * (h + 1)] - acum_t[h:h + 1, :]
                dec = jnp.exp(jnp.where(causal, seg, -jnp.inf))
                ms.append((cb * dec).astype(BF16))
            m_cat = jnp.concatenate(ms, axis=1)
            xdt = x2 * ex_ref[0:L, x0:x0 + LANES]
            x_stack = jnp.concatenate([jnp.where(lo_half, xdt, 0.0).astype(BF16),
                                       jnp.where(lo_half, 0.0, xdt).astype(BF16)], axis=0)
            y2 = _dot(m_cat, x_stack)
            y2 = (y2 + inter[:, LANES * jj:LANES * (jj + 1)] * ex_ref[L:2 * L, x0:x0 + LANES]
                  + dsk_ref[:, x0:x0 + LANES] * x2)
            us.append(y2 * _silu(z_g[:, LANES * jj:LANES * (jj + 1)]))
            xws.append(x2 * ex_ref[2 * L:3 * L, x0:x0 + LANES])
        ss = (jnp.sum(us[0] * us[0], axis=-1, keepdims=True)
              + jnp.sum(us[1] * us[1], axis=-1, keepdims=True))
        scale = lax.rsqrt(ss * (1.0 / gw) + RMS_EPS)
        for jj in range(2):
            x0 = gw * g + LANES * jj
            y_ref[:, x0:x0 + LANES] = (us[jj] * scale * g_ref[:, x0:x0 + LANES]).astype(y_ref.dtype)
        xw_t = jnp.concatenate(xws, axis=1).T
        e_rows = jnp.concatenate(
            [jnp.broadcast_to(e_last_t[SSD_HPG * g + r:SSD_HPG * g + r + 1, :], (SSD_HEAD_DIM, D_STATE))
             for r in range(SSD_HPG)], axis=0)
        so_ref[gw * g:gw * (g + 1), :] = h_g * e_rows + _dot(xw_t.astype(BF16), b_g)


def _ssd_branch(x, conv_state8, ssm0, w, n_valid_last):
    bt, t, _ = x.shape
    n_c = t // CHUNK
    rows = CHUNK * (SSD_CHUNKS_PER_STEP if n_c % SSD_CHUNKS_PER_STEP == 0 else 1)
    kern = functools.partial(_ssd_kernel, n_valid_last=n_valid_last)
    once = lambda shape: pl.BlockSpec(shape, lambda b, i: (0, 0), pipeline_mode=pl.Buffered(1))
    return pl.pallas_call(
        kern,
        grid=(bt, t // rows),
        in_specs=[
            pl.BlockSpec((None, rows, D_MODEL), lambda b, i: (b, i, 0)),
            pl.BlockSpec((None, SUBLANES, CONV_DIM), lambda b, i: (b, 0, 0)),
            pl.BlockSpec((None, D_INNER, D_STATE), lambda b, i: (b, 0, 0)),
            once((D_MODEL, D_INNER)),
            once((D_MODEL, CONV_DIM)),
            _full((D_MODEL, LANES)),
            _full((CONV_W, CONV_DIM)),
            _full((1, CONV_DIM)),
            _full((1, LANES)),
            _full((1, LANES)),
            _full((1, D_INNER)),
            _full((1, D_INNER)),
            once((2 * LANES, D_INNER)),
            once((2 * LANES, SSD_HEADS * D_STATE)),
        ],
        out_specs=[
            pl.BlockSpec((None, rows, D_INNER), lambda b, i: (b, i, 0)),
            pl.BlockSpec((None, SUBLANES, CONV_DIM), lambda b, i: (b, 0, 0)),
            pl.BlockSpec((None, D_INNER, D_STATE), lambda b, i: (b, 0, 0)),
        ],
        out_shape=[
            jax.ShapeDtypeStruct((bt, t, D_INNER), BF16),
            jax.ShapeDtypeStruct((bt, SUBLANES, CONV_DIM), F32),
            jax.ShapeDtypeStruct((bt, D_INNER, D_STATE), F32),
        ],
        scratch_shapes=[
            pltpu.VMEM((SUBLANES + rows, CONV_DIM), F32),
            pltpu.VMEM((rows, CONV_DIM), F32),
            pltpu.VMEM((3 * CHUNK, D_INNER), F32),
            pltpu.VMEM((CHUNK, SSD_HEADS * D_STATE), F32),
        ],
        compiler_params=pltpu.CompilerParams(
            dimension_semantics=("parallel", "arbitrary"), vmem_limit_bytes=VMEM_LIMIT),
        name="ssd_branch",
    )(x, conv_state8, ssm0, w["w_z"], w["w_xbc"], w["w_dt"], w["conv_w"], w["conv_b"], w["dt_bias"],
      w["a_log"], w["d_skip"], w["ssd_norm_g"], _head_spread_matrix(SSD_HEAD_DIM),
      _head_spread_matrix(D_STATE))


def _qkv_kernel(x_ref, w_ref, q_ref, k_ref, v_ref, kb_ref, vb_ref):
    xb = x_ref[...].astype(BF16)
    q_ref[...] = (_dot(xb, w_ref[:, 0:SB_WIDTH]) * (SB_SCALE * LOG2E)).astype(BF16)
    k = _dot(xb, w_ref[:, SB_WIDTH:2 * SB_WIDTH])
    k_ref[...] = k
    kb_ref[...] = k.astype(BF16)
    v = _dot(xb, w_ref[:, 2 * SB_WIDTH:3 * SB_WIDTH])
    v_ref[...] = v
    vb_ref[...] = v.astype(BF16)


def _qkv_proj(x2d, w_qkv, tm):
    t = x2d.shape[0]
    row = lambda i: (i, 0)
    return pl.pallas_call(
        _qkv_kernel,
        grid=(t // tm,),
        in_specs=[pl.BlockSpec((tm, D_MODEL), row), _full((D_MODEL, 3 * SB_WIDTH))],
        out_specs=[pl.BlockSpec((tm, SB_WIDTH), row)] * 5,
        out_shape=[
            jax.ShapeDtypeStruct((t, SB_WIDTH), BF16),
            jax.ShapeDtypeStruct((t, SB_WIDTH), F32),
            jax.ShapeDtypeStruct((t, SB_WIDTH), F32),
            jax.ShapeDtypeStruct((t, SB_WIDTH), BF16),
            jax.ShapeDtypeStruct((t, SB_WIDTH), BF16),
        ],
        compiler_params=pltpu.CompilerParams(
            dimension_semantics=("parallel",), vmem_limit_bytes=VMEM_LIMIT),
        name="qkv_proj",
    )(x2d, w_qkv)


def _qkv_prompt_kernel(x_ref, w_ref, q_ref, kt_ref, vt_ref, ktb_ref, vst_ref):
    xb = x_ref[...].astype(BF16)
    t = xb.shape[0]
    q_ref[...] = (_dot(xb, w_ref[:, 0:SB_WIDTH]) * (SB_SCALE * LOG2E)).astype(BF16)
    k_t = _dot(xb, w_ref[:, SB_WIDTH:2 * SB_WIDTH]).T
    kt_ref[...] = k_t
    ktb_ref[...] = k_t.astype(BF16)
    v = _dot(xb, w_ref[:, 2 * SB_WIDTH:3 * SB_WIDTH])
    vt_ref[...] = v.T
    lo_half = lax.broadcasted_iota(jnp.int32, (t, LANES), 1) < SB_HEAD_DIM
    for hp in range(SB_WIDTH // LANES):
        v2 = v[:, LANES * hp:LANES * (hp + 1)]
        vst_ref[hp, 0:t, :] = jnp.where(lo_half, v2, 0.0).astype(BF16)
        vst_ref[hp, t:2 * t, :] = jnp.where(lo_half, 0.0, v2).astype(BF16)


def _qkv_prompt(x, w_qkv):
    b, s, _ = x.shape
    t = SB_TILE
    n_hp = SB_WIDTH // LANES
    return pl.pallas_call(
        _qkv_prompt_kernel,
        grid=(b, s // t),
        in_specs=[pl.BlockSpec((None, t, D_MODEL), lambda bi, i: (bi, i, 0)),
                  _full((D_MODEL, 3 * SB_WIDTH))],
        out_specs=[
            pl.BlockSpec((None, t, SB_WIDTH), lambda bi, i: (bi, i, 0)),
            pl.BlockSpec((None, SB_WIDTH, t), lambda bi, i: (bi, 0, i)),
            pl.BlockSpec((None, SB_WIDTH, t), lambda bi, i: (bi, 0, i)),
            pl.BlockSpec((None, SB_WIDTH, t), lambda bi, i: (bi, 0, i)),
            pl.BlockSpec((None, None, n_hp, 2 * t, LANES), lambda bi, i: (bi, i, 0, 0, 0)),
        ],
        out_shape=[
            jax.ShapeDtypeStruct((b, s, SB_WIDTH), BF16),
            jax.ShapeDtypeStruct((b, SB_WIDTH, s), F32),
            jax.ShapeDtypeStruct((b, SB_WIDTH, s), F32),
            jax.ShapeDtypeStruct((b, SB_WIDTH, s), BF16),
            jax.ShapeDtypeStruct((b, s // t, n_hp, 2 * t, LANES), BF16),
        ],
        compiler_params=pltpu.CompilerParams(
            dimension_semantics=("parallel", "parallel"), vmem_limit_bytes=VMEM_LIMIT),
        name="qkv_prompt",
    )(x, w_qkv)


def _suffix_matrix(t):
    r = lax.broadcasted_iota(jnp.int32, (t, t), 0)
    c = lax.broadcasted_iota(jnp.int32, (t, t), 1)
    return (r > c).astype(BF16)


def _neg_abs(x):
    bits = lax.bitcast_convert_type(x, jnp.uint32) | jnp.uint32(0x80000000)
    return lax.bitcast_convert_type(bits, F32)


def _sp2(z2, valid):
    sp = jnp.maximum(z2, 0.0) + jnp.log(1.0 + jnp.exp2(_neg_abs(z2))) * LOG2E
    return sp if valid is None else jnp.where(valid, sp, 0.0)


def _sb_tile(z2, u, valid):
    sp = _sp2(z2, valid)
    suffix = _dot(sp.astype(BF16), u)
    a = jnp.exp2(z2 - sp - suffix)
    if valid is not None:
        a = jnp.where(valid, a, 0.0)
    return a.astype(BF16), jnp.sum(sp, axis=-1, keepdims=True)


def _sbp_kernel(bias_ref, q_ref, k_ref, v_ref, u_ref, o_ref):
    _sbp_body(pl.program_id(1), pl.program_id(2), bias_ref, q_ref, k_ref, v_ref, u_ref, o_ref)


def _sbp_body(hp, qi, bias_ref, q_ref, k_ref, v_ref, u_ref, o_ref):
    t = SB_TILE
    n_sub = q_ref.shape[0] // t
    lane = lax.broadcasted_iota(jnp.int32, (t, LANES), 1)
    lo_half = lane < SB_HEAD_DIM

    def bias_lanes(b):
        bv = jnp.full((t, LANES), b * LOG2E, F32)
        hi = bv.astype(BF16).astype(F32)
        return jnp.where(lane == 0, hi, jnp.where(lane == 1, bv - hi, 0.0)).astype(BF16)

    ext = jnp.concatenate([bias_lanes(bias_ref[2 * hp]), bias_lanes(bias_ref[2 * hp + 1])], axis=0)
    qs = []
    for s in range(n_sub):
        q2 = q_ref[s * t:(s + 1) * t, :].astype(F32)
        q_heads = jnp.concatenate([jnp.where(lo_half, q2, 0.0), jnp.where(lo_half, 0.0, q2)],
                                  axis=0).astype(BF16)
        qs.append(jnp.concatenate([q_heads, ext], axis=1))
    ones_rows = (lax.broadcasted_iota(jnp.int32, (LANES, t), 0) < 2).astype(BF16)
    u = u_ref[...]
    r = lax.broadcasted_iota(jnp.int32, (t, t), 0)
    c = lax.broadcasted_iota(jnp.int32, (t, t), 1)
    diag_valid = c < r

    def chain(z, ls_pair, v_tile, valid):
        a0, rs0 = _sb_tile(z[0:t], u, valid)
        a1, rs1 = _sb_tile(z[t:2 * t], u, valid)
        out = _dot(jnp.concatenate([a0, a1], axis=1), v_tile)
        w = jnp.where(lo_half, jnp.exp2(-ls_pair[0]), jnp.exp2(-ls_pair[1]))
        return out * w, (ls_pair[0] + rs0, ls_pair[1] + rs1)

    def tile(kt, carry, valids):
        accs, lss = list(carry[0]), list(carry[1])
        k0 = pl.multiple_of(kt * t, t)
        v_tile = v_ref[kt]
        live = [s for s in range(n_sub) if not isinstance(valids[s], str)]
        k_ext = jnp.concatenate([k_ref[:, pl.ds(k0, t)], ones_rows], axis=0)
        z_all = _dot(jnp.concatenate([qs[s] for s in live], axis=0), k_ext)
        for i, s in enumerate(live):
            d, lss[s] = chain(z_all[2 * t * i:2 * t * (i + 1)], lss[s], v_tile, valids[s])
            accs[s] = accs[s] + d
        return tuple(accs), tuple(lss)

    zero_col = jnp.zeros((t, 1), F32)
    carry = (tuple(jnp.zeros((t, LANES), F32) for _ in range(n_sub)),
             tuple((zero_col, zero_col) for _ in range(n_sub)))
    for d in range(n_sub - 1, -1, -1):
        valids = ["skip" if s < d else (diag_valid if s == d else None) for s in range(n_sub)]
        carry = tile(n_sub * qi + d, carry, valids)
    n_full = n_sub * qi
    accs, _ = lax.fori_loop(0, n_full, lambda i, cr: tile(n_full - 1 - i, cr, [None] * n_sub), carry)
    for s in range(n_sub):
        o_ref[s * t:(s + 1) * t, :] = accs[s].astype(o_ref.dtype)


def _sb_prompt(q, ktb, vst, sb_bias):
    b, s, _ = q.shape
    t = SB_TILE
    tq = SB_Q_SUBBLOCKS * t
    n_hp = SB_WIDTH // LANES
    grid_spec = pltpu.PrefetchScalarGridSpec(
        num_scalar_prefetch=1,
        grid=(b, n_hp, s // tq),
        in_specs=[
            pl.BlockSpec((None, tq, LANES), lambda bi, hp, qi, bias: (bi, qi, hp)),
            pl.BlockSpec((None, LANES, s), lambda bi, hp, qi, bias: (bi, hp, 0)),
            pl.BlockSpec((None, s // t, None, 2 * t, LANES), lambda bi, hp, qi, bias: (bi, 0, hp, 0, 0)),
            pl.BlockSpec((t, t), lambda bi, hp, qi, bias: (0, 0)),
        ],
        out_specs=pl.BlockSpec((None, tq, LANES), lambda bi, hp, qi, bias: (bi, qi, hp)),
    )
    return pl.pallas_call(
        _sbp_kernel,
        grid_spec=grid_spec,
        out_shape=jax.ShapeDtypeStruct((b, s, SB_WIDTH), BF16),
        compiler_params=pltpu.CompilerParams(
            dimension_semantics=("parallel", "parallel", "arbitrary"), vmem_limit_bytes=VMEM_LIMIT),
        name="sb_prompt",
    )(sb_bias, q, ktb, vst, _suffix_matrix(t))


def _sbs_kernel(pt_ref, q_ref, kn_ref, vn_ref, bias_ref, u_ref, *rest, n_tok, n_pp):
    kp_refs, vp_refs = rest[0:n_pp], rest[n_pp:2 * n_pp]
    o_ref, acc, ls = rest[2 * n_pp:]
    _sbs_body(pl.program_id(1), pl.num_programs(1), q_ref, kn_ref, vn_ref, bias_ref, u_ref, kp_refs, vp_refs,
              o_ref, acc, ls, n_tok)


def _sbs_body(j, n_j, q_ref, kn_ref, vn_ref, bias_ref, u_ref, kp_refs, vp_refs, o_ref, acc, ls, n_tok):
    n_pp = len(kp_refs)
    n_rows = q_ref.shape[0]
    row = lax.broadcasted_iota(jnp.int32, (n_rows, SB_WIDTH), 0)
    lane = lax.broadcasted_iota(jnp.int32, (n_rows, SB_WIDTH), 1)
    head_mask = (row & (SB_HEADS - 1)) == (lane >> 6)
    q_bd = jnp.where(head_mask, q_ref[...].astype(F32), 0.0).astype(BF16)
    bias2 = bias_ref[...] * LOG2E
    u2 = u_ref[...]

    @pl.when(j == 0)
    def _():
        tok = lax.broadcasted_iota(jnp.int32, (n_rows, PAGE), 0) >> 4
        key = lax.broadcasted_iota(jnp.int32, (n_rows, PAGE), 1)
        a, rs = _sb_tile(_dot_nt(q_bd, kn_ref[...]) + bias2, u2[0:PAGE, 0:PAGE], key < tok)
        acc[...] = _dot(a, vn_ref[...])
        ls[...] = rs

    k_cat = jnp.concatenate([r[...].astype(BF16) for r in kp_refs], axis=1)
    z = _dot(q_bd, k_cat) + jnp.concatenate([bias2] * n_pp, axis=1)
    sp = _sp2(z, None)
    off = ls[...]
    a_parts = []
    for pr in range(n_pp // 2):
        sl = slice(2 * PAGE * pr, 2 * PAGE * (pr + 1))
        zs = z[:, sl] - sp[:, sl] - _dot(sp[:, sl].astype(BF16), u2)
        for p in range(2):
            ps = slice(PAGE * p, PAGE * (p + 1))
            a_parts.append(jnp.exp2(zs[:, ps] - off).astype(BF16))
            off = off + jnp.sum(sp[:, sl][:, ps], axis=-1, keepdims=True)
    v_cat = jnp.concatenate([r[...].astype(BF16) for r in vp_refs], axis=1)
    acc[...] += _dot_nt(jnp.concatenate(a_parts, axis=1), v_cat)
    ls[...] = off

    @pl.when(j == n_j - 1)
    def _():
        own = jnp.where(head_mask, acc[...], 0.0)
        o_ref[...] = own.reshape(n_tok, SB_HEADS, SB_WIDTH).sum(axis=1)


def _pages_per_step(n_pages):
    for p in range(MAX_PAGES_PER_STEP, 0, -2):
        if n_pages % p == 0:
            return p
    raise ValueError("the page count must be even")


def _sample_operands(q, k_new, v_new, sb_bias):
    n_tok = q.shape[1]
    n_rows = n_tok * SB_HEADS
    q_rep = jnp.repeat(q, SB_HEADS, axis=1)
    pad = ((0, 0), (0, PAGE - n_tok), (0, 0))
    bias_rows = jnp.broadcast_to(jnp.tile(sb_bias, n_tok)[:, None], (n_rows, PAGE)).astype(F32)
    u1 = _suffix_matrix(PAGE)
    zero = jnp.zeros_like(u1)
    u2 = jnp.concatenate([jnp.concatenate([u1, zero], axis=1), jnp.concatenate([zero, u1], axis=1)], axis=0)
    return q_rep, jnp.pad(k_new, pad), jnp.pad(v_new, pad), bias_rows, u2


def _sb_fused_kernel(bias_ref, pt_ref, q_ref, k_ref, v_ref, u_ref, qs_ref, kn_ref, vn_ref, brow_ref, u2_ref,
                     *rest, n_tok, n_pp, steps_per_batch):
    kp_refs, vp_refs = rest[0:n_pp], rest[n_pp:2 * n_pp]
    o_ref, os_ref, acc, ls = rest[2 * n_pp:]
    step = (pl.program_id(0) * pl.num_programs(1) + pl.program_id(1)) * pl.num_programs(2) + pl.program_id(2)
    _sbs_body(lax.rem(step, steps_per_batch), steps_per_batch, qs_ref, kn_ref, vn_ref, brow_ref, u2_ref,
              kp_refs, vp_refs, os_ref, acc, ls, n_tok)
    _sbp_body(pl.program_id(1), pl.program_id(2), bias_ref, q_ref, k_ref, v_ref, u_ref, o_ref)


def _fused_pages_per_step(n_prompt_steps, bt, n_pages):
    total = bt * n_pages
    if total % n_prompt_steps:
        return None
    n_pp = total // n_prompt_steps
    return n_pp if (n_pp % 2 == 0 and n_pages % n_pp == 0) else None


def _sb_fused(q, ktb, vst, sb_bias, qs, k_new, v_new, cache_kt, cache_vt, page_table, n_pp):
    b, s, _ = q.shape
    t = SB_TILE
    tq = SB_Q_SUBBLOCKS * t
    n_hp = SB_WIDTH // LANES
    n_q = s // tq
    bt, n_tok, _ = qs.shape
    n_pages = page_table.shape[1]
    spb = n_pages // n_pp
    n_rows = n_tok * SB_HEADS
    q_rep, kn, vn, bias_rows, u2 = _sample_operands(qs, k_new, v_new, sb_bias)
    step = lambda bi, hp, qi: (bi * n_hp + hp) * n_q + qi
    per_sb = lambda bi, hp, qi, bias, pt: (step(bi, hp, qi) // spb, 0, 0)
    const = lambda bi, hp, qi, bias, pt: (0, 0)

    def page(p):
        def index(bi, hp, qi, bias, pt):
            st = step(bi, hp, qi)
            return (pt[st // spb, n_pages - 1 - ((st % spb) * n_pp + p)], 0, 0)
        return index

    page_specs = [pl.BlockSpec((None, SB_WIDTH, PAGE), page(p)) for p in range(n_pp)]
    grid_spec = pltpu.PrefetchScalarGridSpec(
        num_scalar_prefetch=2,
        grid=(b, n_hp, n_q),
        in_specs=[
            pl.BlockSpec((None, tq, LANES), lambda bi, hp, qi, bias, pt: (bi, qi, hp)),
            pl.BlockSpec((None, LANES, s), lambda bi, hp, qi, bias, pt: (bi, hp, 0)),
            pl.BlockSpec((None, s // t, None, 2 * t, LANES), lambda bi, hp, qi, bias, pt: (bi, 0, hp, 0, 0)),
            pl.BlockSpec((t, t), const),
            pl.BlockSpec((None, n_rows, SB_WIDTH), per_sb),
            pl.BlockSpec((None, PAGE, SB_WIDTH), per_sb),
            pl.BlockSpec((None, PAGE, SB_WIDTH), per_sb),
            pl.BlockSpec((n_rows, PAGE), const),
            pl.BlockSpec((2 * PAGE, 2 * PAGE), const),
        ] + page_specs + page_specs,
        out_specs=[
            pl.BlockSpec((None, tq, LANES), lambda bi, hp, qi, bias, pt: (bi, qi, hp)),
            pl.BlockSpec((None, n_tok, SB_WIDTH), per_sb),
        ],
        scratch_shapes=[pltpu.VMEM((n_rows, SB_WIDTH), F32), pltpu.VMEM((n_rows, 1), F32)],
    )
    return pl.pallas_call(
        functools.partial(_sb_fused_kernel, n_tok=n_tok, n_pp=n_pp, steps_per_batch=spb),
        grid_spec=grid_spec,
        out_shape=[jax.ShapeDtypeStruct((b, s, SB_WIDTH), BF16),
                   jax.ShapeDtypeStruct((bt, n_tok, SB_WIDTH), F32)],
        compiler_params=pltpu.CompilerParams(
            dimension_semantics=("arbitrary", "arbitrary", "arbitrary"), vmem_limit_bytes=VMEM_LIMIT),
        name="sb_fused",
    )(sb_bias, page_table, q, ktb, vst, _suffix_matrix(t), q_rep, kn, vn, bias_rows, u2,
      *([cache_kt] * n_pp), *([cache_vt] * n_pp))


def _sb_sample(q, k_new, v_new, cache_kt, cache_vt, page_table, sb_bias):
    bt, n_tok, _ = q.shape
    n_pages = page_table.shape[1]
    n_pp = _pages_per_step(n_pages)
    n_rows = n_tok * SB_HEADS
    q_rep, kn, vn, bias_rows, u2 = _sample_operands(q, k_new, v_new, sb_bias)
    per_b = lambda b, j, pt: (b, 0, 0)
    const = lambda b, j, pt: (0, 0)

    def page(p):
        return lambda b, j, pt: (pt[b, n_pages - 1 - (j * n_pp + p)], 0, 0)

    page_specs = [pl.BlockSpec((None, SB_WIDTH, PAGE), page(p)) for p in range(n_pp)]
    grid_spec = pltpu.PrefetchScalarGridSpec(
        num_scalar_prefetch=1,
        grid=(bt, n_pages // n_pp),
        in_specs=[
            pl.BlockSpec((None, n_rows, SB_WIDTH), per_b),
            pl.BlockSpec((None, PAGE, SB_WIDTH), per_b),
            pl.BlockSpec((None, PAGE, SB_WIDTH), per_b),
            pl.BlockSpec((n_rows, PAGE), const),
            pl.BlockSpec((2 * PAGE, 2 * PAGE), const),
        ] + page_specs + page_specs,
        out_specs=pl.BlockSpec((None, n_tok, SB_WIDTH), per_b),
        scratch_shapes=[pltpu.VMEM((n_rows, SB_WIDTH), F32), pltpu.VMEM((n_rows, 1), F32)],
    )
    return pl.pallas_call(
        functools.partial(_sbs_kernel, n_tok=n_tok, n_pp=n_pp),
        grid_spec=grid_spec,
        out_shape=jax.ShapeDtypeStruct((bt, n_tok, SB_WIDTH), F32),
        compiler_params=pltpu.CompilerParams(
            dimension_semantics=("parallel", "arbitrary"), vmem_limit_bytes=VMEM_LIMIT),
        name="sb_sample",
    )(page_table, q_rep, kn, vn, bias_rows, u2, *([cache_kt] * n_pp), *([cache_vt] * n_pp))


def _merge_kernel(x_ref, y_ref, o_ref, wg_ref, wso_ref, wbo_ref, wo_ref, g_ref, b_ref, h_ref):
    x = x_ref[...]
    xb = x.astype(BF16)
    ga = _dot(xb, wg_ref[:, 0:D_MODEL])
    gb = _dot(xb, wg_ref[:, D_MODEL:2 * D_MODEL])
    p_ssd = _dot(y_ref[...].astype(BF16), wso_ref[...])
    p_sb = _dot(o_ref[...].astype(BF16), wbo_ref[...])
    merged = jax.nn.sigmoid(ga) * p_ssd + jax.nn.sigmoid(gb) * p_sb
    pre = DN_ALPHA * x + _dot(merged.astype(BF16), wo_ref[...])
    h_ref[...] = _layer_norm(pre, g_ref[...], b_ref[...])


def _merge_ln(x2d, y_ssd, o_sb, w, tm):
    t = x2d.shape[0]
    row = lambda i: (i, 0)
    return pl.pallas_call(
        _merge_kernel,
        grid=(t // tm,),
        in_specs=[
            pl.BlockSpec((tm, D_MODEL), row),
            pl.BlockSpec((tm, D_INNER), row),
            pl.BlockSpec((tm, SB_WIDTH), row),
            _full((D_MODEL, 2 * D_MODEL)),
            _full((D_INNER, D_MODEL)),
            _full((SB_WIDTH, D_MODEL)),
            _full((D_MODEL, D_MODEL)),
            _full((1, D_MODEL)),
            _full((1, D_MODEL)),
        ],
        out_specs=pl.BlockSpec((tm, D_MODEL), row),
        out_shape=jax.ShapeDtypeStruct((t, D_MODEL), F32),
        compiler_params=pltpu.CompilerParams(
            dimension_semantics=("parallel",), vmem_limit_bytes=VMEM_LIMIT),
        name="merge_ln",
    )(x2d, y_ssd, o_sb, w["w_gate"], w["w_ssd_out"], w["w_sb_out"], w["w_out"], w["ln1_g"], w["ln1_b"])


def _ffn_kernel(h_ref, pe_ref, wgu_ref, wd_ref, wpg_ref, wpp_ref, g_ref, b_ref, o_ref):
    h = h_ref[...]
    hb = h.astype(BF16)
    g_ = _dot(hb, wgu_ref[:, 0:D_FF])
    u_ = _dot(hb, wgu_ref[:, D_FF:2 * D_FF])
    ffn = _dot((_silu(g_) * u_).astype(BF16), wd_ref[...])
    ple = jax.nn.sigmoid(_dot(hb, wpg_ref[...])) * _dot(pe_ref[...].astype(BF16), wpp_ref[...])
    o_ref[...] = _layer_norm(DN_ALPHA * h + ffn + ple, g_ref[...], b_ref[...])


def _ffn_ln(h2d, pe2d, w, tm):
    t = h2d.shape[0]
    row = lambda i: (i, 0)
    once = pl.Buffered(1)
    return pl.pallas_call(
        _ffn_kernel,
        grid=(t // tm,),
        in_specs=[
            pl.BlockSpec((tm, D_MODEL), row),
            pl.BlockSpec((tm, PLE_DIM), row),
            pl.BlockSpec((D_MODEL, 2 * D_FF), lambda i: (0, 0), pipeline_mode=once),
            pl.BlockSpec((D_FF, D_MODEL), lambda i: (0, 0), pipeline_mode=once),
            pl.BlockSpec((D_MODEL, D_MODEL), lambda i: (0, 0), pipeline_mode=once),
            pl.BlockSpec((PLE_DIM, D_MODEL), lambda i: (0, 0), pipeline_mode=once),
            _full((1, D_MODEL)),
            _full((1, D_MODEL)),
        ],
        out_specs=pl.BlockSpec((tm, D_MODEL), row),
        out_shape=jax.ShapeDtypeStruct((t, D_MODEL), F32),
        compiler_params=pltpu.CompilerParams(
            dimension_semantics=("parallel",), vmem_limit_bytes=VMEM_LIMIT),
        name="ffn_ln",
    )(h2d, pe2d, w["w_gu"], w["w_down"], w["w_ple_gate"], w["w_ple_proj"], w["ln2_g"], w["ln2_b"])


def _prep_weights(w_in, conv_w, conv_b, dt_bias, a_log, d_skip, ssd_norm_g, w_ssd_out, w_sb_out, w_out,
                  ln1_g, ln1_b, w_gu, w_down, w_ple_gate, w_ple_proj, ln2_g, ln2_b):
    offs = [0]
    for s in _IN_SIZES:
        offs.append(offs[-1] + s)
    seg = lambda k: w_in[:, offs[k]:offs[k + 1]]
    lane_pad = lambda a: jnp.pad(a, ((0, 0), (0, LANES - a.shape[1])))
    return dict(
        w_z=seg(0).astype(BF16),
        w_xbc=seg(1).astype(BF16),
        w_dt=lane_pad(seg(2)).astype(BF16),
        w_qkv=jnp.concatenate([seg(3), seg(4), seg(5)], axis=1).astype(BF16),
        w_gate=jnp.concatenate([seg(6), seg(7)], axis=1).astype(BF16),
        conv_w=conv_w, conv_b=conv_b[None, :],
        dt_bias=lane_pad(dt_bias[None, :]), a_log=lane_pad(a_log[None, :]),
        d_skip=jnp.repeat(d_skip, SSD_HEAD_DIM)[None, :], ssd_norm_g=ssd_norm_g[None, :],
        w_ssd_out=w_ssd_out.astype(BF16), w_sb_out=w_sb_out.astype(BF16), w_out=w_out.astype(BF16),
        ln1_g=ln1_g[None, :], ln1_b=ln1_b[None, :],
        w_gu=w_gu.astype(BF16), w_down=w_down.astype(BF16),
        w_ple_gate=w_ple_gate.astype(BF16), w_ple_proj=w_ple_proj.astype(BF16),
        ln2_g=ln2_g[None, :], ln2_b=ln2_b[None, :],
    )


def _tail(x2d, pe2d, y_ssd2d, o_sb2d, w, tm):
    h = _merge_ln(x2d, y_ssd2d, o_sb2d, w, tm)
    return _ffn_ln(h, pe2d, w, tm)


def _heads_last(a_t, bt, s):
    return jnp.transpose(a_t.reshape(bt, SB_HEADS, SB_HEAD_DIM, s), (0, 3, 1, 2))


def _layer(xp, pp, xs, ps, state_conv, state_ssm, cache_k, cache_v, page_table, w, sb_bias, tm=256):
    bp, s, _ = xp.shape
    bs, n_tok, _ = xs.shape
    tp, ts = bp * s, bs * n_tok
    xp2d, xs2d = xp.reshape(tp, D_MODEL), xs.reshape(ts, D_MODEL)

    yp_ssd, conv_p, ssm_p = _ssd_branch(xp, jnp.zeros((bp, SUBLANES, CONV_DIM), F32),
                                        jnp.zeros((bp, D_INNER, D_STATE), F32), w, CHUNK)
    xs_pad = jnp.pad(xs, ((0, 0), (0, CHUNK - n_tok), (0, 0)))
    conv8 = jnp.pad(state_conv, ((0, 0), (SUBLANES - (CONV_W - 1), 0), (0, 0)))
    ys_pad, conv_s, ssm_s = _ssd_branch(xs_pad, conv8, state_ssm.reshape(bs, D_INNER, D_STATE), w, n_tok)
    ys_ssd = ys_pad[:, :n_tok].reshape(ts, D_INNER)

    qp, kp_t, vp_t, ktb, vst = _qkv_prompt(xp, w["w_qkv"])
    qs, ks, vs, ksb, vsb = _qkv_proj(xs2d, w["w_qkv"], ts)
    shp = (bs, n_tok, SB_WIDTH)
    qs, ksb, vsb = qs.reshape(shp), ksb.reshape(shp), vsb.reshape(shp)
    n_pool = cache_k.shape[0]
    cache_kt = jnp.transpose(cache_k, (0, 2, 3, 1)).reshape(n_pool, SB_WIDTH, PAGE)
    cache_vt = jnp.transpose(cache_v, (0, 2, 3, 1)).reshape(n_pool, SB_WIDTH, PAGE)
    n_prompt_steps = bp * (SB_WIDTH // LANES) * (s // (SB_Q_SUBBLOCKS * SB_TILE))
    n_pp = _fused_pages_per_step(n_prompt_steps, bs, page_table.shape[1])
    if n_pp is None:
        op_sb = _sb_prompt(qp, ktb, vst, sb_bias)
        os_sb = _sb_sample(qs, ksb, vsb, cache_kt, cache_vt, page_table, sb_bias)
    else:
        op_sb, os_sb = _sb_fused(qp, ktb, vst, sb_bias, qs, ksb, vsb, cache_kt, cache_vt, page_table, n_pp)

    out_p = _tail(xp2d, pp.reshape(tp, PLE_DIM), yp_ssd.reshape(tp, D_INNER), op_sb.reshape(tp, SB_WIDTH), w, tm)
    out_s = _tail(xs2d, ps.reshape(ts, PLE_DIM), ys_ssd, os_sb.reshape(ts, SB_WIDTH), w, ts)
    tail_rows = slice(SUBLANES - (CONV_W - 1), SUBLANES)
    heads = lambda a, bt, n: a.reshape(bt, n, SB_HEADS, SB_HEAD_DIM)
    return (out_p.reshape(bp, s, D_MODEL), out_s.reshape(bs, n_tok, D_MODEL),
            _heads_last(kp_t, bp, s), _heads_last(vp_t, bp, s), conv_p[:, tail_rows],
            ssm_p.reshape(bp, SSD_HEADS, SSD_HEAD_DIM, D_STATE),
            heads(ks, bs, n_tok), heads(vs, bs, n_tok), conv_s[:, tail_rows],
            ssm_s.reshape(bs, SSD_HEADS, SSD_HEAD_DIM, D_STATE))


def kernel(x_prompt, x_sample, cache_k, cache_v, state_conv, state_ssm, page_table, p_prompt, p_sample,
           w_in, conv_w, conv_b, dt_bias, a_log, d_skip, ssd_norm_g, sb_bias, w_ssd_out, w_sb_out, w_out,
           ln1_g, ln1_b, w_gu, w_down, w_ple_gate, w_ple_proj, ln2_g, ln2_b):
    assert w_in.shape[0] == DEPTH
    w = _prep_weights(w_in[0], conv_w[0], conv_b[0], dt_bias[0], a_log[0], d_skip[0], ssd_norm_g[0],
                      w_ssd_out[0], w_sb_out[0], w_out[0], ln1_g[0], ln1_b[0], w_gu[0], w_down[0],
                      w_ple_gate[0], w_ple_proj[0], ln2_g[0], ln2_b[0])
    outs = _layer(x_prompt, p_prompt[0], x_sample, p_sample[0], state_conv[0], state_ssm[0],
                  cache_k[0], cache_v[0], page_table, w, sb_bias[0])
    yp, ys = outs[0], outs[1]
    return (yp, ys) + tuple(o[None] for o in outs[2:])
```

```python
import functools

import jax
import jax.numpy as jnp
from jax import lax
from jax.experimental import pallas as pl
from jax.experimental.pallas import tpu as pltpu

F32 = jnp.float32
BF16 = jnp.bfloat16

D_MODEL = 1024
D_INNER = 2048
SSD_HEADS = 32
SSD_GROUPS = 8
SSD_HPG = SSD_HEADS // SSD_GROUPS
SSD_HEAD_DIM = 64
D_STATE = 128
CONV_W = 4
CONV_DIM = D_INNER + 2 * SSD_GROUPS * D_STATE
CHUNK = 128
SSD_CHUNKS_PER_STEP = 2
SSD_LOOKAHEAD = 2
SB_HEADS = 16
SB_HEAD_DIM = 64
SB_WIDTH = SB_HEADS * SB_HEAD_DIM
SB_SCALE = SB_HEAD_DIM ** -0.5
PAGE = 128
SB_TILE = 256
SB_Q_SUBBLOCKS = 4
MAX_PAGES_PER_STEP = 8
LOG2E = 1.4426950408889634
Z2_MAX = 120.0
D_FF = 2816
PLE_DIM = 256
LN_EPS = 1e-5
RMS_EPS = 1e-5
DEPTH = 1
DN_ALPHA = (2 * DEPTH) ** 0.25

LANES = 128
SUBLANES = 8
VMEM_LIMIT = 56 * 1024 * 1024

_IN_SIZES = (D_INNER, CONV_DIM, SSD_HEADS, SB_WIDTH, SB_WIDTH, SB_WIDTH, D_MODEL, D_MODEL)


def _dot(a, b):
    return jnp.dot(a, b, preferred_element_type=F32)


def _dot_nt(a, b):
    return lax.dot_general(a, b, (((1,), (1,)), ((), ())), preferred_element_type=F32)


def _softplus(x):
    return jnp.maximum(x, 0.0) + jnp.log1p(jnp.exp(-jnp.abs(x)))


def _silu(x):
    return x * jax.nn.sigmoid(x)


def _split3(x):
    hi = x.astype(BF16)
    r = x - hi.astype(F32)
    mid = r.astype(BF16)
    lo = (r - mid.astype(F32)).astype(BF16)
    return hi, mid, lo


def _layer_norm(x, g, b):
    mu = jnp.mean(x, axis=-1, keepdims=True)
    d = x - mu
    var = jnp.mean(d * d, axis=-1, keepdims=True)
    return d * lax.rsqrt(var + LN_EPS) * g + b


def _full(shape):
    n = len(shape)
    return pl.BlockSpec(shape, lambda *_: (0,) * n)


def _ssd_kernel(x_ref, cs_ref, s0_ref, wz_ref, wx_ref, wdt_ref, cw_ref, cb_ref, dtb_ref, alog_ref,
                dsk_ref, g_ref, e64_ref, e128_ref, y_ref, co_ref, so_ref, xpad, act, ex_ref, seg_ref, *,
                n_valid_last):
    i = pl.program_id(1)
    n_i = pl.num_programs(1)
    L = CHUNK
    rows = x_ref.shape[0]

    @pl.when(i == 0)
    def _():
        xpad[0:SUBLANES, :] = cs_ref[...]
        so_ref[...] = s0_ref[...]

    xb_all = x_ref[...].astype(BF16)

    xpad[SUBLANES:SUBLANES + rows, :] = _dot(xb_all, wx_ref[...])

    for c in range(0, CONV_DIM, 512):
        sl = slice(c, c + 512)
        xp = xpad[:, sl]
        conv = cb_ref[:, sl] + cw_ref[CONV_W - 1:CONV_W, sl] * xp[SUBLANES:, :]
        for s in range(1, CONV_W):
            shifted = pltpu.roll(xp, s, 0)[SUBLANES:, :]
            conv = conv + cw_ref[CONV_W - 1 - s:CONV_W - s, sl] * shifted
        act[:, sl] = _silu(conv)

    @pl.when(i == n_i - 1)
    def _():
        last = rows - L + n_valid_last
        co_ref[...] = xpad[last:last + SUBLANES, :]

    xpad[0:SUBLANES, :] = xpad[rows:rows + SUBLANES, :]

    dt_all = _softplus(_dot(xb_all, wdt_ref[...]) + dtb_ref[...])
    for ci in range(rows // L):
        last_chunk = (i == n_i - 1) if (ci == rows // L - 1 and n_valid_last < L) else False
        _ssd_chunk(L * ci, last_chunk, n_valid_last, xb_all[L * ci:L * (ci + 1)], dt_all[L * ci:L * (ci + 1)],
                   wz_ref, alog_ref, dsk_ref, g_ref, e64_ref, e128_ref, y_ref, so_ref, act, ex_ref, seg_ref)


def _spread(x, e_ref):
    hi = x.astype(BF16)
    lo = (x - hi.astype(F32)).astype(BF16)
    return _dot(jnp.concatenate([hi, lo], axis=1), e_ref[...])


def _head_spread_matrix(width):
    h = lax.broadcasted_iota(jnp.int32, (LANES, SSD_HEADS * width), 0)
    c = lax.broadcasted_iota(jnp.int32, (LANES, SSD_HEADS * width), 1)
    e = (c // width == h).astype(BF16)
    return jnp.concatenate([e, e], axis=0)


def _ssd_chunk(r0, last_chunk, n_valid_last, xb, dt, wz_ref, alog_ref, dsk_ref, g_ref, e64_ref, e128_ref,
               y_ref, so_ref, act_ref, ex_ref, seg_ref):
    L = CHUNK
    act = act_ref.at[r0:r0 + L, :]
    y_ref = y_ref.at[r0:r0 + L, :]
    row = lax.broadcasted_iota(jnp.int32, (L, LANES), 0)
    col = lax.broadcasted_iota(jnp.int32, (L, LANES), 1)
    if last_chunk is not False:
        limit = jnp.where(last_chunk, n_valid_last, L)
        dt = jnp.where(row < limit, dt, 0.0)
    d_a = dt * (-jnp.exp(alog_ref[...]))
    causal = row >= col
    tril = causal.astype(BF16)
    hi, mid, lo = _split3(d_a)
    acum = _dot(tril, hi) + _dot(tril, mid) + _dot(tril, lo)
    acum_t = acum.T
    e_last_t = jnp.exp(acum_t[:, L - 1:L])
    per_head = jnp.concatenate([dt, jnp.exp(acum), jnp.exp(acum[L - 1:L, :] - acum) * dt], axis=0)
    ex_ref[...] = _spread(per_head, e64_ref)
    seg_ref[...] = _spread(acum, e128_ref)
    lo_half = col < SSD_HEAD_DIM
    gw = SSD_HPG * SSD_HEAD_DIM

    def group_matmuls(g):
        b_g = act[:, D_INNER + D_STATE * g:D_INNER + D_STATE * (g + 1)].astype(BF16)
        c0 = D_INNER + SSD_GROUPS * D_STATE + D_STATE * g
        c_g = act[:, c0:c0 + D_STATE].astype(BF16)
        cb = _dot_nt(c_g, b_g)
        h_g = so_ref[gw * g:gw * (g + 1), :]
        inter = _dot_nt(c_g, h_g.astype(BF16))
        z_g = _dot(xb, wz_ref[:, gw * g:gw * (g + 1)])
        return b_g, cb, h_g, inter, z_g

    ahead = [group_matmuls(g) for g in range(SSD_LOOKAHEAD)]
    for g in range(SSD_GROUPS):
        b_g, cb, h_g, inter, z_g = ahead.pop(0)
        if g + SSD_LOOKAHEAD < SSD_GROUPS:
            ahead.append(group_matmuls(g + SSD_LOOKAHEAD))
        us, xws = [], []
        for jj in range(2):
            h0 = SSD_HPG * g + 2 * jj
            x0 = gw * g + LANES * jj
            x2 = act[:, x0:x0 + LANES]
            ms = []
            for h in (h0, h0 + 1):
                seg = seg_ref[:, D_STATE * h:D_STATE * (h + 1)] - acum_t[h:h + 1, :]
                dec = jnp.exp(jnp.where(causal, seg, -jnp.inf))
                ms.append((cb * dec).astype(BF16))
            xdt = x2 * ex_ref[0:L, x0:x0 + LANES]
            x_stack = jnp.concatenate([jnp.where(lo_half, xdt, 0.0).astype(BF16),
                                       jnp.where(lo_half, 0.0, xdt).astype(BF16)], axis=0)
            y2 = _dot(jnp.concatenate(ms, axis=1), x_stack)
            y2 = (y2 + inter[:, LANES * jj:LANES * (jj + 1)] * ex_ref[L:2 * L, x0:x0 + LANES]
                  + dsk_ref[:, x0:x0 + LANES] * x2)
            us.append(y2 * _silu(z_g[:, LANES * jj:LANES * (jj + 1)]))
            xws.append(x2 * ex_ref[2 * L:3 * L, x0:x0 + LANES])
        ss = (jnp.sum(us[0] * us[0], axis=-1, keepdims=True)
              + jnp.sum(us[1] * us[1], axis=-1, keepdims=True))
        scale = lax.rsqrt(ss * (1.0 / gw) + RMS_EPS)
        for jj in range(2):
            x0 = gw * g + LANES * jj
            y_ref[:, x0:x0 + LANES] = (us[jj] * scale * g_ref[:, x0:x0 + LANES]).astype(y_ref.dtype)
        xw_t = jnp.concatenate(xws, axis=1).T
        e_rows = jnp.concatenate(
            [jnp.broadcast_to(e_last_t[SSD_HPG * g + r:SSD_HPG * g + r + 1, :], (SSD_HEAD_DIM, D_STATE))
             for r in range(SSD_HPG)], axis=0)
        so_ref[gw * g:gw * (g + 1), :] = h_g * e_rows + _dot(xw_t.astype(BF16), b_g)


def _ssd_branch(x, conv_state8, ssm0, w, n_valid_last):
    bt, t, _ = x.shape
    n_c = t // CHUNK
    rows = CHUNK * (SSD_CHUNKS_PER_STEP if n_c % SSD_CHUNKS_PER_STEP == 0 else 1)
    kern = functools.partial(_ssd_kernel, n_valid_last=n_valid_last)
    once = lambda shape: pl.BlockSpec(shape, lambda b, i: (0, 0), pipeline_mode=pl.Buffered(1))
    return pl.pallas_call(
        kern,
        grid=(bt, t // rows),
        in_specs=[
            pl.BlockSpec((None, rows, D_MODEL), lambda b, i: (b, i, 0)),
            pl.BlockSpec((None, SUBLANES, CONV_DIM), lambda b, i: (b, 0, 0)),
            pl.BlockSpec((None, D_INNER, D_STATE), lambda b, i: (b, 0, 0)),
            once((D_MODEL, D_INNER)),
            once((D_MODEL, CONV_DIM)),
            _full((D_MODEL, LANES)),
            _full((CONV_W, CONV_DIM)),
            _full((1, CONV_DIM)),
            _full((1, LANES)),
            _full((1, LANES)),
            _full((1, D_INNER)),
            _full((1, D_INNER)),
            once((2 * LANES, D_INNER)),
            once((2 * LANES, SSD_HEADS * D_STATE)),
        ],
        out_specs=[
            pl.BlockSpec((None, rows, D_INNER), lambda b, i: (b, i, 0)),
            pl.BlockSpec((None, SUBLANES, CONV_DIM), lambda b, i: (b, 0, 0)),
            pl.BlockSpec((None, D_INNER, D_STATE), lambda b, i: (b, 0, 0)),
        ],
        out_shape=[
            jax.ShapeDtypeStruct((bt, t, D_INNER), BF16),
            jax.ShapeDtypeStruct((bt, SUBLANES, CONV_DIM), F32),
            jax.ShapeDtypeStruct((bt, D_INNER, D_STATE), F32),
        ],
        scratch_shapes=[
            pltpu.VMEM((SUBLANES + rows, CONV_DIM), F32),
            pltpu.VMEM((rows, CONV_DIM), F32),
            pltpu.VMEM((3 * CHUNK, D_INNER), F32),
            pltpu.VMEM((CHUNK, SSD_HEADS * D_STATE), F32),
        ],
        compiler_params=pltpu.CompilerParams(
            dimension_semantics=("parallel", "arbitrary"), vmem_limit_bytes=VMEM_LIMIT),
        name="ssd_branch",
    )(x, conv_state8, ssm0, w["w_z"], w["w_xbc"], w["w_dt"], w["conv_w"], w["conv_b"], w["dt_bias"],
      w["a_log"], w["d_skip"], w["ssd_norm_g"], _head_spread_matrix(SSD_HEAD_DIM),
      _head_spread_matrix(D_STATE))


def _qkv_kernel(x_ref, w_ref, q_ref, k_ref, v_ref, kb_ref, vb_ref):
    xb = x_ref[...].astype(BF16)
    q_ref[...] = (_dot(xb, w_ref[:, 0:SB_WIDTH]) * (SB_SCALE * LOG2E)).astype(BF16)
    k = _dot(xb, w_ref[:, SB_WIDTH:2 * SB_WIDTH])
    k_ref[...] = k
    kb_ref[...] = k.astype(BF16)
    v = _dot(xb, w_ref[:, 2 * SB_WIDTH:3 * SB_WIDTH])
    v_ref[...] = v
    vb_ref[...] = v.astype(BF16)


def _qkv_proj(x2d, w_qkv, tm):
    t = x2d.shape[0]
    row = lambda i: (i, 0)
    return pl.pallas_call(
        _qkv_kernel,
        grid=(t // tm,),
        in_specs=[pl.BlockSpec((tm, D_MODEL), row), _full((D_MODEL, 3 * SB_WIDTH))],
        out_specs=[pl.BlockSpec((tm, SB_WIDTH), row)] * 5,
        out_shape=[
            jax.ShapeDtypeStruct((t, SB_WIDTH), BF16),
            jax.ShapeDtypeStruct((t, SB_WIDTH), F32),
            jax.ShapeDtypeStruct((t, SB_WIDTH), F32),
            jax.ShapeDtypeStruct((t, SB_WIDTH), BF16),
            jax.ShapeDtypeStruct((t, SB_WIDTH), BF16),
        ],
        compiler_params=pltpu.CompilerParams(
            dimension_semantics=("parallel",), vmem_limit_bytes=VMEM_LIMIT),
        name="qkv_proj",
    )(x2d, w_qkv)


def _qkv_prompt_kernel(x_ref, w_ref, q_ref, kt_ref, vt_ref, ktb_ref, vst_ref):
    xb = x_ref[...].astype(BF16)
    t = xb.shape[0]
    q_ref[...] = (_dot(xb, w_ref[:, 0:SB_WIDTH]) * (SB_SCALE * LOG2E)).astype(BF16)
    k_t = _dot(xb, w_ref[:, SB_WIDTH:2 * SB_WIDTH]).T
    kt_ref[...] = k_t
    ktb_ref[...] = k_t.astype(BF16)
    v = _dot(xb, w_ref[:, 2 * SB_WIDTH:3 * SB_WIDTH])
    vt_ref[...] = v.T
    lo_half = lax.broadcasted_iota(jnp.int32, (t, LANES), 1) < SB_HEAD_DIM
    for hp in range(SB_WIDTH // LANES):
        v2 = v[:, LANES * hp:LANES * (hp + 1)]
        vst_ref[hp, 0:t, :] = jnp.where(lo_half, v2, 0.0).astype(BF16)
        vst_ref[hp, t:2 * t, :] = jnp.where(lo_half, 0.0, v2).astype(BF16)


def _qkv_prompt(x, w_qkv):
    b, s, _ = x.shape
    t = SB_TILE
    n_hp = SB_WIDTH // LANES
    return pl.pallas_call(
        _qkv_prompt_kernel,
        grid=(b, s // t),
        in_specs=[pl.BlockSpec((None, t, D_MODEL), lambda bi, i: (bi, i, 0)),
                  _full((D_MODEL, 3 * SB_WIDTH))],
        out_specs=[
            pl.BlockSpec((None, t, SB_WIDTH), lambda bi, i: (bi, i, 0)),
            pl.BlockSpec((None, SB_WIDTH, t), lambda bi, i: (bi, 0, i)),
            pl.BlockSpec((None, SB_WIDTH, t), lambda bi, i: (bi, 0, i)),
            pl.BlockSpec((None, SB_WIDTH, t), lambda bi, i: (bi, 0, i)),
            pl.BlockSpec((None, None, n_hp, 2 * t, LANES), lambda bi, i: (bi, i, 0, 0, 0)),
        ],
        out_shape=[
            jax.ShapeDtypeStruct((b, s, SB_WIDTH), BF16),
            jax.ShapeDtypeStruct((b, SB_WIDTH, s), F32),
            jax.ShapeDtypeStruct((b, SB_WIDTH, s), F32),
            jax.ShapeDtypeStruct((b, SB_WIDTH, s), BF16),
            jax.ShapeDtypeStruct((b, s // t, n_hp, 2 * t, LANES), BF16),
        ],
        compiler_params=pltpu.CompilerParams(
            dimension_semantics=("parallel", "parallel"), vmem_limit_bytes=VMEM_LIMIT),
        name="qkv_prompt",
    )(x, w_qkv)


def _suffix_matrix(t):
    r = lax.broadcasted_iota(jnp.int32, (t, t), 0)
    c = lax.broadcasted_iota(jnp.int32, (t, t), 1)
    return (r > c).astype(BF16)


def _clamp_logits(z2):
    return jnp.minimum(z2, Z2_MAX)


def _sp2(z2, valid):
    sp = jnp.log(1.0 + jnp.exp2(z2)) * LOG2E
    return sp if valid is None else jnp.where(valid, sp, 0.0)


def _sb_tile_sums(z2, u, valid):
    z2 = _clamp_logits(z2)
    sp = _sp2(z2, valid)
    return z2, sp, _dot(sp.astype(BF16), u)


def _sb_tile_weights(z2, sp, suffix, valid):
    a = jnp.exp2(z2 - sp - suffix)
    if valid is not None:
        a = jnp.where(valid, a, 0.0)
    return a.astype(BF16), jnp.sum(sp, axis=-1, keepdims=True)


def _sb_tile(z2, u, valid):
    return _sb_tile_weights(*_sb_tile_sums(z2, u, valid), valid)


def _sbp_kernel(bias_ref, q_ref, k_ref, v_ref, u_ref, o_ref):
    _sbp_body(pl.program_id(1), pl.program_id(2), bias_ref, q_ref, k_ref, v_ref, u_ref, o_ref)


def _sbp_body(hp, qi, bias_ref, q_ref, k_ref, v_ref, u_ref, o_ref, after_diagonal=None):
    t = SB_TILE
    n_sub = q_ref.shape[0] // t
    lane = lax.broadcasted_iota(jnp.int32, (t, LANES), 1)
    lo_half = lane < SB_HEAD_DIM

    def bias_lanes(b):
        bv = jnp.full((t, LANES), b * LOG2E, F32)
        hi = bv.astype(BF16).astype(F32)
        return jnp.where(lane == 0, hi, jnp.where(lane == 1, bv - hi, 0.0)).astype(BF16)

    ext = jnp.concatenate([bias_lanes(bias_ref[2 * hp]), bias_lanes(bias_ref[2 * hp + 1])], axis=0)
    qs = []
    for s in range(n_sub):
        q2 = q_ref[s * t:(s + 1) * t, :].astype(F32)
        q_heads = jnp.concatenate([jnp.where(lo_half, q2, 0.0), jnp.where(lo_half, 0.0, q2)],
                                  axis=0).astype(BF16)
        qs.append(jnp.concatenate([q_heads, ext], axis=1))
    ones_rows = (lax.broadcasted_iota(jnp.int32, (LANES, t), 0) < 2).astype(BF16)
    u = u_ref[...]
    r = lax.broadcasted_iota(jnp.int32, (t, t), 0)
    c = lax.broadcasted_iota(jnp.int32, (t, t), 1)
    diag_valid = c < r

    def sums(z, valid):
        return _sb_tile_sums(z[0:t], u, valid), _sb_tile_sums(z[t:2 * t], u, valid)

    def finish(halves, ls2, v_tile, valid):
        a0, rs0 = _sb_tile_weights(*halves[0], valid)
        a1, rs1 = _sb_tile_weights(*halves[1], valid)
        out = _dot(jnp.concatenate([a0, a1], axis=1), v_tile)
        return out * jnp.exp2(-ls2), ls2 + jnp.where(lo_half, rs0, rs1)

    def tile(kt, carry, valids):
        accs, lss = list(carry[0]), list(carry[1])
        k0 = pl.multiple_of(kt * t, t)
        v_tile = v_ref[kt]
        live = [s for s in range(n_sub) if not isinstance(valids[s], str)]
        k_ext = jnp.concatenate([k_ref[:, pl.ds(k0, t)], ones_rows], axis=0)
        z_all = _dot(jnp.concatenate([qs[s] for s in live], axis=0), k_ext)
        z_of = lambda i: z_all[2 * t * i:2 * t * (i + 1)]
        ahead = sums(z_of(0), valids[live[0]])
        for i, s in enumerate(live):
            halves = ahead
            if i + 1 < len(live):
                ahead = sums(z_of(i + 1), valids[live[i + 1]])
            d, lss[s] = finish(halves, lss[s], v_tile, valids[s])
            accs[s] = accs[s] + d
        return tuple(accs), tuple(lss)

    carry = (tuple(jnp.zeros((t, LANES), F32) for _ in range(n_sub)),
             tuple(jnp.zeros((t, LANES), F32) for _ in range(n_sub)))
    for d in range(n_sub - 1, -1, -1):
        valids = ["skip" if s < d else (diag_valid if s == d else None) for s in range(n_sub)]
        carry = tile(n_sub * qi + d, carry, valids)
    if after_diagonal is not None:
        after_diagonal()
    n_full = n_sub * qi
    accs, _ = lax.fori_loop(0, n_full, lambda i, cr: tile(n_full - 1 - i, cr, [None] * n_sub), carry)
    for s in range(n_sub):
        o_ref[s * t:(s + 1) * t, :] = accs[s].astype(o_ref.dtype)


def _sb_prompt(q, ktb, vst, sb_bias):
    b, s, _ = q.shape
    t = SB_TILE
    tq = SB_Q_SUBBLOCKS * t
    n_hp = SB_WIDTH // LANES
    grid_spec = pltpu.PrefetchScalarGridSpec(
        num_scalar_prefetch=1,
        grid=(b, n_hp, s // tq),
        in_specs=[
            pl.BlockSpec((None, tq, LANES), lambda bi, hp, qi, bias: (bi, qi, hp)),
            pl.BlockSpec((None, LANES, s), lambda bi, hp, qi, bias: (bi, hp, 0)),
            pl.BlockSpec((None, s // t, None, 2 * t, LANES), lambda bi, hp, qi, bias: (bi, 0, hp, 0, 0)),
            pl.BlockSpec((t, t), lambda bi, hp, qi, bias: (0, 0)),
        ],
        out_specs=pl.BlockSpec((None, tq, LANES), lambda bi, hp, qi, bias: (bi, qi, hp)),
    )
    return pl.pallas_call(
        _sbp_kernel,
        grid_spec=grid_spec,
        out_shape=jax.ShapeDtypeStruct((b, s, SB_WIDTH), BF16),
        compiler_params=pltpu.CompilerParams(
            dimension_semantics=("parallel", "parallel", "arbitrary"), vmem_limit_bytes=VMEM_LIMIT),
        name="sb_prompt",
    )(sb_bias, q, ktb, vst, _suffix_matrix(t))


def _sbs_kernel(pt_ref, q_ref, kn_ref, vn_ref, bias_ref, u_ref, *rest, n_tok, n_pp):
    kp_refs, vp_refs = rest[0:n_pp], rest[n_pp:2 * n_pp]
    o_ref, acc, ls = rest[2 * n_pp:]
    _sbs_body(pl.program_id(1), pl.num_programs(1), q_ref, kn_ref, vn_ref, bias_ref, u_ref, kp_refs, vp_refs,
              o_ref, acc, ls, n_tok)


def _sbs_body(j, n_j, q_ref, kn_ref, vn_ref, bias_ref, u_ref, kp_refs, vp_refs, o_ref, acc, ls, n_tok,
              place_sweep=None):
    n_pp = len(kp_refs)
    n_rows = q_ref.shape[0]
    row = lax.broadcasted_iota(jnp.int32, (n_rows, SB_WIDTH), 0)
    lane = lax.broadcasted_iota(jnp.int32, (n_rows, SB_WIDTH), 1)
    head_mask = (row & (SB_HEADS - 1)) == (lane >> 6)
    q_bd = jnp.where(head_mask, q_ref[...].astype(F32), 0.0).astype(BF16)
    bias2 = bias_ref[...] * LOG2E
    u2 = u_ref[...]

    @pl.when(j == 0)
    def _():
        tok = lax.broadcasted_iota(jnp.int32, (n_rows, PAGE), 0) >> 4
        key = lax.broadcasted_iota(jnp.int32, (n_rows, PAGE), 1)
        a, rs = _sb_tile(_dot_nt(q_bd, kn_ref[...]) + bias2, u2[0:PAGE, 0:PAGE], key < tok)
        acc[...] = _dot(a, vn_ref[...])
        ls[...] = rs

    def sweep():
        k_cat = jnp.concatenate([r[...].astype(BF16) for r in kp_refs], axis=1)
        z = _clamp_logits(_dot(q_bd, k_cat) + jnp.concatenate([bias2] * n_pp, axis=1))
        sp = _sp2(z, None)
        off = ls[...]
        a_parts = []
        for pr in range(n_pp // 2):
            sl = slice(2 * PAGE * pr, 2 * PAGE * (pr + 1))
            zs = z[:, sl] - sp[:, sl] - _dot(sp[:, sl].astype(BF16), u2)
            for p in range(2):
                ps = slice(PAGE * p, PAGE * (p + 1))
                a_parts.append(jnp.exp2(zs[:, ps] - off).astype(BF16))
                off = off + jnp.sum(sp[:, sl][:, ps], axis=-1, keepdims=True)
        v_cat = jnp.concatenate([r[...].astype(BF16) for r in vp_refs], axis=1)
        acc[...] += _dot_nt(jnp.concatenate(a_parts, axis=1), v_cat)
        ls[...] = off

    if place_sweep is None:
        sweep()
    else:
        place_sweep(sweep)

    @pl.when(j == n_j - 1)
    def _():
        own = jnp.where(head_mask, acc[...], 0.0)
        o_ref[...] = own.reshape(n_tok, SB_HEADS, SB_WIDTH).sum(axis=1)


def _pages_per_step(n_pages):
    for p in range(MAX_PAGES_PER_STEP, 0, -2):
        if n_pages % p == 0:
            return p
    raise ValueError("the page count must be even")


def _sample_operands(q, k_new, v_new, sb_bias):
    n_tok = q.shape[1]
    n_rows = n_tok * SB_HEADS
    q_rep = jnp.repeat(q, SB_HEADS, axis=1)
    pad = ((0, 0), (0, PAGE - n_tok), (0, 0))
    bias_rows = jnp.broadcast_to(jnp.tile(sb_bias, n_tok)[:, None], (n_rows, PAGE)).astype(F32)
    u1 = _suffix_matrix(PAGE)
    zero = jnp.zeros_like(u1)
    u2 = jnp.concatenate([jnp.concatenate([u1, zero], axis=1), jnp.concatenate([zero, u1], axis=1)], axis=0)
    return q_rep, jnp.pad(k_new, pad), jnp.pad(v_new, pad), bias_rows, u2


def _sb_fused_kernel(bias_ref, pt_ref, q_ref, k_ref, v_ref, u_ref, qs_ref, kn_ref, vn_ref, brow_ref, u2_ref,
                     *rest, n_tok, n_pp, steps_per_batch):
    kp_refs, vp_refs = rest[0:n_pp], rest[n_pp:2 * n_pp]
    o_ref, os_ref, acc, ls = rest[2 * n_pp:]
    step = (pl.program_id(0) * pl.num_programs(1) + pl.program_id(1)) * pl.num_programs(2) + pl.program_id(2)
    hp, qi = pl.program_id(1), pl.program_id(2)
    _sbs_body(lax.rem(step, steps_per_batch), steps_per_batch, qs_ref, kn_ref, vn_ref, brow_ref, u2_ref,
              kp_refs, vp_refs, os_ref, acc, ls, n_tok,
              place_sweep=lambda sweep: _sbp_body(hp, qi, bias_ref, q_ref, k_ref, v_ref, u_ref, o_ref,
                                                  after_diagonal=sweep))


def _fused_pages_per_step(n_prompt_steps, bt, n_pages):
    total = bt * n_pages
    if total % n_prompt_steps:
        return None
    n_pp = total // n_prompt_steps
    return n_pp if (n_pp % 2 == 0 and n_pages % n_pp == 0) else None


def _sb_fused(q, ktb, vst, sb_bias, qs, k_new, v_new, cache_kt, cache_vt, page_table, n_pp):
    b, s, _ = q.shape
    t = SB_TILE
    tq = SB_Q_SUBBLOCKS * t
    n_hp = SB_WIDTH // LANES
    n_q = s // tq
    bt, n_tok, _ = qs.shape
    n_pages = page_table.shape[1]
    spb = n_pages // n_pp
    n_rows = n_tok * SB_HEADS
    q_rep, kn, vn, bias_rows, u2 = _sample_operands(qs, k_new, v_new, sb_bias)
    step = lambda bi, hp, qi: (bi * n_hp + hp) * n_q + qi
    per_sb = lambda bi, hp, qi, bias, pt: (step(bi, hp, qi) // spb, 0, 0)
    const = lambda bi, hp, qi, bias, pt: (0, 0)

    def page(p):
        def index(bi, hp, qi, bias, pt):
            st = step(bi, hp, qi)
            return (pt[st // spb, n_pages - 1 - ((st % spb) * n_pp + p)], 0, 0)
        return index

    page_specs = [pl.BlockSpec((None, SB_WIDTH, PAGE), page(p)) for p in range(n_pp)]
    grid_spec = pltpu.PrefetchScalarGridSpec(
        num_scalar_prefetch=2,
        grid=(b, n_hp, n_q),
        in_specs=[
            pl.BlockSpec((None, tq, LANES), lambda bi, hp, qi, bias, pt: (bi, qi, hp)),
            pl.BlockSpec((None, LANES, s), lambda bi, hp, qi, bias, pt: (bi, hp, 0)),
            pl.BlockSpec((None, s // t, None, 2 * t, LANES), lambda bi, hp, qi, bias, pt: (bi, 0, hp, 0, 0)),
            pl.BlockSpec((t, t), const),
            pl.BlockSpec((None, n_rows, SB_WIDTH), per_sb),
            pl.BlockSpec((None, PAGE, SB_WIDTH), per_sb),
            pl.BlockSpec((None, PAGE, SB_WIDTH), per_sb),
            pl.BlockSpec((n_rows, PAGE), const),
            pl.BlockSpec((2 * PAGE, 2 * PAGE), const),
        ] + page_specs + page_specs,
        out_specs=[
            pl.BlockSpec((None, tq, LANES), lambda bi, hp, qi, bias, pt: (bi, qi, hp)),
            pl.BlockSpec((None, n_tok, SB_WIDTH), per_sb),
        ],
        scratch_shapes=[pltpu.VMEM((n_rows, SB_WIDTH), F32), pltpu.VMEM((n_rows, 1), F32)],
    )
    return pl.pallas_call(
        functools.partial(_sb_fused_kernel, n_tok=n_tok, n_pp=n_pp, steps_per_batch=spb),
        grid_spec=grid_spec,
        out_shape=[jax.ShapeDtypeStruct((b, s, SB_WIDTH), BF16),
                   jax.ShapeDtypeStruct((bt, n_tok, SB_WIDTH), F32)],
        compiler_params=pltpu.CompilerParams(
            dimension_semantics=("arbitrary", "arbitrary", "arbitrary"), vmem_limit_bytes=VMEM_LIMIT),
        name="sb_fused",
    )(sb_bias, page_table, q, ktb, vst, _suffix_matrix(t), q_rep, kn, vn, bias_rows, u2,
      *([cache_kt] * n_pp), *([cache_vt] * n_pp))


def _sb_sample(q, k_new, v_new, cache_kt, cache_vt, page_table, sb_bias):
    bt, n_tok, _ = q.shape
    n_pages = page_table.shape[1]
    n_pp = _pages_per_step(n_pages)
    n_rows = n_tok * SB_HEADS
    q_rep, kn, vn, bias_rows, u2 = _sample_operands(q, k_new, v_new, sb_bias)
    per_b = lambda b, j, pt: (b, 0, 0)
    const = lambda b, j, pt: (0, 0)

    def page(p):
        return lambda b, j, pt: (pt[b, n_pages - 1 - (j * n_pp + p)], 0, 0)

    page_specs = [pl.BlockSpec((None, SB_WIDTH, PAGE), page(p)) for p in range(n_pp)]
    grid_spec = pltpu.PrefetchScalarGridSpec(
        num_scalar_prefetch=1,
        grid=(bt, n_pages // n_pp),
        in_specs=[
            pl.BlockSpec((None, n_rows, SB_WIDTH), per_b),
            pl.BlockSpec((None, PAGE, SB_WIDTH), per_b),
            pl.BlockSpec((None, PAGE, SB_WIDTH), per_b),
            pl.BlockSpec((n_rows, PAGE), const),
            pl.BlockSpec((2 * PAGE, 2 * PAGE), const),
        ] + page_specs + page_specs,
        out_specs=pl.BlockSpec((None, n_tok, SB_WIDTH), per_b),
        scratch_shapes=[pltpu.VMEM((n_rows, SB_WIDTH), F32), pltpu.VMEM((n_rows, 1), F32)],
    )
    return pl.pallas_call(
        functools.partial(_sbs_kernel, n_tok=n_tok, n_pp=n_pp),
        grid_spec=grid_spec,
        out_shape=jax.ShapeDtypeStruct((bt, n_tok, SB_WIDTH), F32),
        compiler_params=pltpu.CompilerParams(
            dimension_semantics=("parallel", "arbitrary"), vmem_limit_bytes=VMEM_LIMIT),
        name="sb_sample",
    )(page_table, q_rep, kn, vn, bias_rows, u2, *([cache_kt] * n_pp), *([cache_vt] * n_pp))


def _merge_kernel(x_ref, y_ref, o_ref, wg_ref, wso_ref, wbo_ref, wo_ref, g_ref, b_ref, h_ref):
    x = x_ref[...]
    xb = x.astype(BF16)
    ga = _dot(xb, wg_ref[:, 0:D_MODEL])
    gb = _dot(xb, wg_ref[:, D_MODEL:2 * D_MODEL])
    p_ssd = _dot(y_ref[...].astype(BF16), wso_ref[...])
    p_sb = _dot(o_ref[...].astype(BF16), wbo_ref[...])
    merged = jax.nn.sigmoid(ga) * p_ssd + jax.nn.sigmoid(gb) * p_sb
    pre = DN_ALPHA * x + _dot(merged.astype(BF16), wo_ref[...])
    h_ref[...] = _layer_norm(pre, g_ref[...], b_ref[...])


def _merge_ln(x2d, y_ssd, o_sb, w, tm):
    t = x2d.shape[0]
    row = lambda i: (i, 0)
    return pl.pallas_call(
        _merge_kernel,
        grid=(t // tm,),
        in_specs=[
            pl.BlockSpec((tm, D_MODEL), row),
            pl.BlockSpec((tm, D_INNER), row),
            pl.BlockSpec((tm, SB_WIDTH), row),
            _full((D_MODEL, 2 * D_MODEL)),
            _full((D_INNER, D_MODEL)),
            _full((SB_WIDTH, D_MODEL)),
            _full((D_MODEL, D_MODEL)),
            _full((1, D_MODEL)),
            _full((1, D_MODEL)),
        ],
        out_specs=pl.BlockSpec((tm, D_MODEL), row),
        out_shape=jax.ShapeDtypeStruct((t, D_MODEL), F32),
        compiler_params=pltpu.CompilerParams(
            dimension_semantics=("parallel",), vmem_limit_bytes=VMEM_LIMIT),
        name="merge_ln",
    )(x2d, y_ssd, o_sb, w["w_gate"], w["w_ssd_out"], w["w_sb_out"], w["w_out"], w["ln1_g"], w["ln1_b"])


def _ffn_kernel(h_ref, pe_ref, wgu_ref, wd_ref, wpg_ref, wpp_ref, g_ref, b_ref, o_ref):
    h = h_ref[...]
    hb = h.astype(BF16)
    g_ = _dot(hb, wgu_ref[:, 0:D_FF])
    u_ = _dot(hb, wgu_ref[:, D_FF:2 * D_FF])
    ffn = _dot((_silu(g_) * u_).astype(BF16), wd_ref[...])
    ple = jax.nn.sigmoid(_dot(hb, wpg_ref[...])) * _dot(pe_ref[...].astype(BF16), wpp_ref[...])
    o_ref[...] = _layer_norm(DN_ALPHA * h + ffn + ple, g_ref[...], b_ref[...])


def _ffn_ln(h2d, pe2d, w, tm):
    t = h2d.shape[0]
    row = lambda i: (i, 0)
    once = pl.Buffered(1)
    return pl.pallas_call(
        _ffn_kernel,
        grid=(t // tm,),
        in_specs=[
            pl.BlockSpec((tm, D_MODEL), row),
            pl.BlockSpec((tm, PLE_DIM), row),
            pl.BlockSpec((D_MODEL, 2 * D_FF), lambda i: (0, 0), pipeline_mode=once),
            pl.BlockSpec((D_FF, D_MODEL), lambda i: (0, 0), pipeline_mode=once),
            pl.BlockSpec((D_MODEL, D_MODEL), lambda i: (0, 0), pipeline_mode=once),
            pl.BlockSpec((PLE_DIM, D_MODEL), lambda i: (0, 0), pipeline_mode=once),
            _full((1, D_MODEL)),
            _full((1, D_MODEL)),
        ],
        out_specs=pl.BlockSpec((tm, D_MODEL), row),
        out_shape=jax.ShapeDtypeStruct((t, D_MODEL), F32),
        compiler_params=pltpu.CompilerParams(
            dimension_semantics=("parallel",), vmem_limit_bytes=VMEM_LIMIT),
        name="ffn_ln",
    )(h2d, pe2d, w["w_gu"], w["w_down"], w["w_ple_gate"], w["w_ple_proj"], w["ln2_g"], w["ln2_b"])


def _prep_weights(w_in, conv_w, conv_b, dt_bias, a_log, d_skip, ssd_norm_g, w_ssd_out, w_sb_out, w_out,
                  ln1_g, ln1_b, w_gu, w_down, w_ple_gate, w_ple_proj, ln2_g, ln2_b):
    offs = [0]
    for s in _IN_SIZES:
        offs.append(offs[-1] + s)
    seg = lambda k: w_in[:, offs[k]:offs[k + 1]]
    lane_pad = lambda a: jnp.pad(a, ((0, 0), (0, LANES - a.shape[1])))
    return dict(
        w_z=seg(0).astype(BF16),
        w_xbc=seg(1).astype(BF16),
        w_dt=lane_pad(seg(2)).astype(BF16),
        w_qkv=jnp.concatenate([seg(3), seg(4), seg(5)], axis=1).astype(BF16),
        w_gate=jnp.concatenate([seg(6), seg(7)], axis=1).astype(BF16),
        conv_w=conv_w, conv_b=conv_b[None, :],
        dt_bias=lane_pad(dt_bias[None, :]), a_log=lane_pad(a_log[None, :]),
        d_skip=jnp.repeat(d_skip, SSD_HEAD_DIM)[None, :], ssd_norm_g=ssd_norm_g[None, :],
        w_ssd_out=w_ssd_out.astype(BF16), w_sb_out=w_sb_out.astype(BF16), w_out=w_out.astype(BF16),
        ln1_g=ln1_g[None, :], ln1_b=ln1_b[None, :],
        w_gu=w_gu.astype(BF16), w_down=w_down.astype(BF16),
        w_ple_gate=w_ple_gate.astype(BF16), w_ple_proj=w_ple_proj.astype(BF16),
        ln2_g=ln2_g[None, :], ln2_b=ln2_b[None, :],
    )


def _tail(x2d, pe2d, y_ssd2d, o_sb2d, w, tm):
    h = _merge_ln(x2d, y_ssd2d, o_sb2d, w, tm)
    return _ffn_ln(h, pe2d, w, tm)


def _heads_last(a_t, bt, s):
    return jnp.transpose(a_t.reshape(bt, SB_HEADS, SB_HEAD_DIM, s), (0, 3, 1, 2))


def _layer(xp, pp, xs, ps, state_conv, state_ssm, cache_k, cache_v, page_table, w, sb_bias, tm=512):
    bp, s, _ = xp.shape
    bs, n_tok, _ = xs.shape
    tp, ts = bp * s, bs * n_tok
    xp2d, xs2d = xp.reshape(tp, D_MODEL), xs.reshape(ts, D_MODEL)

    yp_ssd, conv_p, ssm_p = _ssd_branch(xp, jnp.zeros((bp, SUBLANES, CONV_DIM), F32),
                                        jnp.zeros((bp, D_INNER, D_STATE), F32), w, CHUNK)
    xs_pad = jnp.pad(xs, ((0, 0), (0, CHUNK - n_tok), (0, 0)))
    conv8 = jnp.pad(state_conv, ((0, 0), (SUBLANES - (CONV_W - 1), 0), (0, 0)))
    ys_pad, conv_s, ssm_s = _ssd_branch(xs_pad, conv8, state_ssm.reshape(bs, D_INNER, D_STATE), w, n_tok)
    ys_ssd = ys_pad[:, :n_tok].reshape(ts, D_INNER)

    qp, kp_t, vp_t, ktb, vst = _qkv_prompt(xp, w["w_qkv"])
    qs, ks, vs, ksb, vsb = _qkv_proj(xs2d, w["w_qkv"], ts)
    shp = (bs, n_tok, SB_WIDTH)
    qs, ksb, vsb = qs.reshape(shp), ksb.reshape(shp), vsb.reshape(shp)
    n_pool = cache_k.shape[0]
    cache_kt = jnp.transpose(cache_k, (0, 2, 3, 1)).reshape(n_pool, SB_WIDTH, PAGE)
    cache_vt = jnp.transpose(cache_v, (0, 2, 3, 1)).reshape(n_pool, SB_WIDTH, PAGE)
    n_prompt_steps = bp * (SB_WIDTH // LANES) * (s // (SB_Q_SUBBLOCKS * SB_TILE))
    n_pp = _fused_pages_per_step(n_prompt_steps, bs, page_table.shape[1])
    if n_pp is None:
        op_sb = _sb_prompt(qp, ktb, vst, sb_bias)
        os_sb = _sb_sample(qs, ksb, vsb, cache_kt, cache_vt, page_table, sb_bias)
    else:
        op_sb, os_sb = _sb_fused(qp, ktb, vst, sb_bias, qs, ksb, vsb, cache_kt, cache_vt, page_table, n_pp)

    out_p = _tail(xp2d, pp.reshape(tp, PLE_DIM), yp_ssd.reshape(tp, D_INNER), op_sb.reshape(tp, SB_WIDTH), w, tm)
    out_s = _tail(xs2d, ps.reshape(ts, PLE_DIM), ys_ssd, os_sb.reshape(ts, SB_WIDTH), w, ts)
    tail_rows = slice(SUBLANES - (CONV_W - 1), SUBLANES)
    heads = lambda a, bt, n: a.reshape(bt, n, SB_HEADS, SB_HEAD_DIM)
    return (out_p.reshape(bp, s, D_MODEL), out_s.reshape(bs, n_tok, D_MODEL),
            _heads_last(kp_t, bp, s), _heads_last(vp_t, bp, s), conv_p[:, tail_rows],
            ssm_p.reshape(bp, SSD_HEADS, SSD_HEAD_DIM, D_STATE),
            heads(ks, bs, n_tok), heads(vs, bs, n_tok), conv_s[:, tail_rows],
            ssm_s.reshape(bs, SSD_HEADS, SSD_HEAD_DIM, D_STATE))


def kernel(x_prompt, x_sample, cache_k, cache_v, state_conv, state_ssm, page_table, p_prompt, p_sample,
           w_in, conv_w, conv_b, dt_bias, a_log, d_skip, ssd_norm_g, sb_bias, w_ssd_out, w_sb_out, w_out,
           ln1_g, ln1_b, w_gu, w_down, w_ple_gate, w_ple_proj, ln2_g, ln2_b):
    assert w_in.shape[0] == DEPTH
    w = _prep_weights(w_in[0], conv_w[0], conv_b[0], dt_bias[0], a_log[0], d_skip[0], ssd_norm_g[0],
                      w_ssd_out[0], w_sb_out[0], w_out[0], ln1_g[0], ln1_b[0], w_gu[0], w_down[0],
                      w_ple_gate[0], w_ple_proj[0], ln2_g[0], ln2_b[0])
    outs = _layer(x_prompt, p_prompt[0], x_sample, p_sample[0], state_conv[0], state_ssm[0],
                  cache_k[0], cache_v[0], page_table, w, sb_bias[0])
    yp, ys = outs[0], outs[1]
    return (yp, ys) + tuple(o[None] for o in outs[2:])
```

```python
import functools

import jax
import jax.numpy as jnp
from jax import lax
from jax.experimental import pallas as pl
from jax.experimental.pallas import tpu as pltpu

F32 = jnp.float32
BF16 = jnp.bfloat16

D_MODEL = 1024
D_INNER = 2048
SSD_HEADS = 32
SSD_GROUPS = 8
SSD_HPG = SSD_HEADS // SSD_GROUPS
SSD_HEAD_DIM = 64
D_STATE = 128
CONV_W = 4
CONV_DIM = D_INNER + 2 * SSD_GROUPS * D_STATE
CHUNK = 128
SSD_CHUNKS_PER_STEP = 2
SSD_LOOKAHEAD = 2
SB_HEADS = 16
SB_HEAD_DIM = 64
SB_WIDTH = SB_HEADS * SB_HEAD_DIM
SB_SCALE = SB_HEAD_DIM ** -0.5
PAGE = 128
SB_TILE = 256
SB_Q_SUBBLOCKS = 4
MAX_PAGES_PER_STEP = 8
LOG2E = 1.4426950408889634
Z2_MAX = 120.0
D_FF = 2816
PLE_DIM = 256
LN_EPS = 1e-5
RMS_EPS = 1e-5
DEPTH = 1
DN_ALPHA = (2 * DEPTH) ** 0.25

LANES = 128
SUBLANES = 8
VMEM_LIMIT = 56 * 1024 * 1024

_IN_SIZES = (D_INNER, CONV_DIM, SSD_HEADS, SB_WIDTH, SB_WIDTH, SB_WIDTH, D_MODEL, D_MODEL)


def _dot(a, b):
    return jnp.dot(a, b, preferred_element_type=F32)


def _dot_nt(a, b):
    return lax.dot_general(a, b, (((1,), (1,)), ((), ())), preferred_element_type=F32)


def _softplus(x):
    return jnp.maximum(x, 0.0) + jnp.log1p(jnp.exp(-jnp.abs(x)))


def _silu(x):
    h = 0.5 * x
    return h + h * jnp.tanh(h)


def _split3(x):
    hi = x.astype(BF16)
    r = x - hi.astype(F32)
    mid = r.astype(BF16)
    lo = (r - mid.astype(F32)).astype(BF16)
    return hi, mid, lo


def _layer_norm(x, g, b):
    mu = jnp.mean(x, axis=-1, keepdims=True)
    d = x - mu
    var = jnp.mean(d * d, axis=-1, keepdims=True)
    return d * lax.rsqrt(var + LN_EPS) * g + b


def _full(shape):
    n = len(shape)
    return pl.BlockSpec(shape, lambda *_: (0,) * n)


def _ssd_kernel(x_ref, cs_ref, s0_ref, wz_ref, wx_ref, wdt_ref, cw_ref, cb_ref, dtb_ref, alog_ref,
                dsk_ref, g_ref, e64_ref, e128_ref, y_ref, co_ref, so_ref, xpad, act, ex_ref, seg_ref, *,
                n_valid_last):
    i = pl.program_id(1)
    n_i = pl.num_programs(1)
    L = CHUNK
    rows = x_ref.shape[0]

    @pl.when(i == 0)
    def _():
        xpad[0:SUBLANES, :] = cs_ref[...]
        so_ref[...] = s0_ref[...]

    xb_all = x_ref[...].astype(BF16)

    xpad[SUBLANES:SUBLANES + rows, :] = _dot(xb_all, wx_ref[...])

    for c in range(0, CONV_DIM, 512):
        sl = slice(c, c + 512)
        xp = xpad[:, sl]
        conv = cb_ref[:, sl] + cw_ref[CONV_W - 1:CONV_W, sl] * xp[SUBLANES:, :]
        for s in range(1, CONV_W):
            shifted = pltpu.roll(xp, s, 0)[SUBLANES:, :]
            conv = conv + cw_ref[CONV_W - 1 - s:CONV_W - s, sl] * shifted
        act[:, sl] = _silu(conv)

    @pl.when(i == n_i - 1)
    def _():
        last = rows - L + n_valid_last
        co_ref[...] = xpad[last:last + SUBLANES, :]

    xpad[0:SUBLANES, :] = xpad[rows:rows + SUBLANES, :]

    dt_all = _softplus(_dot(xb_all, wdt_ref[...]) + dtb_ref[...])
    for ci in range(rows // L):
        last_chunk = (i == n_i - 1) if (ci == rows // L - 1 and n_valid_last < L) else False
        _ssd_chunk(L * ci, last_chunk, n_valid_last, xb_all[L * ci:L * (ci + 1)], dt_all[L * ci:L * (ci + 1)],
                   wz_ref, alog_ref, dsk_ref, g_ref, e64_ref, e128_ref, y_ref, so_ref, act, ex_ref, seg_ref)


def _spread(x, e_ref):
    hi = x.astype(BF16)
    lo = (x - hi.astype(F32)).astype(BF16)
    return _dot(jnp.concatenate([hi, lo], axis=1), e_ref[...])


def _head_spread_matrix(width):
    h = lax.broadcasted_iota(jnp.int32, (LANES, SSD_HEADS * width), 0)
    c = lax.broadcasted_iota(jnp.int32, (LANES, SSD_HEADS * width), 1)
    e = (c // width == h).astype(BF16)
    return jnp.concatenate([e, e], axis=0)


def _ssd_chunk(r0, last_chunk, n_valid_last, xb, dt, wz_ref, alog_ref, dsk_ref, g_ref, e64_ref, e128_ref,
               y_ref, so_ref, act_ref, ex_ref, seg_ref):
    L = CHUNK
    act = act_ref.at[r0:r0 + L, :]
    y_ref = y_ref.at[r0:r0 + L, :]
    row = lax.broadcasted_iota(jnp.int32, (L, LANES), 0)
    col = lax.broadcasted_iota(jnp.int32, (L, LANES), 1)
    if last_chunk is not False:
        limit = jnp.where(last_chunk, n_valid_last, L)
        dt = jnp.where(row < limit, dt, 0.0)
    d_a = dt * (-jnp.exp(alog_ref[...]) * LOG2E)
    causal = row >= col
    tril = causal.astype(BF16)
    hi, mid, lo = _split3(d_a)
    acum = _dot(tril, hi) + _dot(tril, mid) + _dot(tril, lo)
    acum_t = acum.T
    e_last_t = jnp.exp2(acum_t[:, L - 1:L])
    per_head = jnp.concatenate([dt, jnp.exp2(acum), jnp.exp2(acum[L - 1:L, :] - acum) * dt], axis=0)
    ex_ref[...] = _spread(per_head, e64_ref)
    seg_ref[...] = _spread(acum, e128_ref)
    lo_half = col < SSD_HEAD_DIM
    gw = SSD_HPG * SSD_HEAD_DIM

    def group_matmuls(g):
        b_g = act[:, D_INNER + D_STATE * g:D_INNER + D_STATE * (g + 1)].astype(BF16)
        c0 = D_INNER + SSD_GROUPS * D_STATE + D_STATE * g
        c_g = act[:, c0:c0 + D_STATE].astype(BF16)
        cb = _dot_nt(c_g, b_g)
        h_g = so_ref[gw * g:gw * (g + 1), :]
        inter = _dot_nt(c_g, h_g.astype(BF16))
        z_g = _dot(xb, wz_ref[:, gw * g:gw * (g + 1)])
        return b_g, cb, h_g, inter, z_g

    ahead = [group_matmuls(g) for g in range(SSD_LOOKAHEAD)]
    for g in range(SSD_GROUPS):
        b_g, cb, h_g, inter, z_g = ahead.pop(0)
        if g + SSD_LOOKAHEAD < SSD_GROUPS:
            ahead.append(group_matmuls(g + SSD_LOOKAHEAD))
        us, xws = [], []
        for jj in range(2):
            h0 = SSD_HPG * g + 2 * jj
            x0 = gw * g + LANES * jj
            x2 = act[:, x0:x0 + LANES]
            ms = []
            for h in (h0, h0 + 1):
                seg = seg_ref[:, D_STATE * h:D_STATE * (h + 1)] - acum_t[h:h + 1, :]
                dec = jnp.exp2(jnp.where(causal, seg, -jnp.inf))
                ms.append((cb * dec).astype(BF16))
            xdt = x2 * ex_ref[0:L, x0:x0 + LANES]
            x_stack = jnp.concatenate([jnp.where(lo_half, xdt, 0.0).astype(BF16),
                                       jnp.where(lo_half, 0.0, xdt).astype(BF16)], axis=0)
            y2 = _dot(jnp.concatenate(ms, axis=1), x_stack)
            y2 = (y2 + inter[:, LANES * jj:LANES * (jj + 1)] * ex_ref[L:2 * L, x0:x0 + LANES]
                  + dsk_ref[:, x0:x0 + LANES] * x2)
            us.append(y2 * _silu(z_g[:, LANES * jj:LANES * (jj + 1)]))
            xws.append(x2 * ex_ref[2 * L:3 * L, x0:x0 + LANES])
        ss = (jnp.sum(us[0] * us[0], axis=-1, keepdims=True)
              + jnp.sum(us[1] * us[1], axis=-1, keepdims=True))
        scale = lax.rsqrt(ss * (1.0 / gw) + RMS_EPS)
        for jj in range(2):
            x0 = gw * g + LANES * jj
            y_ref[:, x0:x0 + LANES] = (us[jj] * scale * g_ref[:, x0:x0 + LANES]).astype(y_ref.dtype)
        xw_t = jnp.concatenate(xws, axis=1).T
        e_rows = jnp.concatenate(
            [jnp.broadcast_to(e_last_t[SSD_HPG * g + r:SSD_HPG * g + r + 1, :], (SSD_HEAD_DIM, D_STATE))
             for r in range(SSD_HPG)], axis=0)
        so_ref[gw * g:gw * (g + 1), :] = h_g * e_rows + _dot(xw_t.astype(BF16), b_g)


def _ssd_branch(x, conv_state8, ssm0, w, n_valid_last):
    bt, t, _ = x.shape
    n_c = t // CHUNK
    rows = CHUNK * (SSD_CHUNKS_PER_STEP if n_c % SSD_CHUNKS_PER_STEP == 0 else 1)
    kern = functools.partial(_ssd_kernel, n_valid_last=n_valid_last)
    once = lambda shape: pl.BlockSpec(shape, lambda b, i: (0, 0), pipeline_mode=pl.Buffered(1))
    return pl.pallas_call(
        kern,
        grid=(bt, t // rows),
        in_specs=[
            pl.BlockSpec((None, rows, D_MODEL), lambda b, i: (b, i, 0)),
            pl.BlockSpec((None, SUBLANES, CONV_DIM), lambda b, i: (b, 0, 0)),
            pl.BlockSpec((None, D_INNER, D_STATE), lambda b, i: (b, 0, 0)),
            once((D_MODEL, D_INNER)),
            once((D_MODEL, CONV_DIM)),
            _full((D_MODEL, LANES)),
            _full((CONV_W, CONV_DIM)),
            _full((1, CONV_DIM)),
            _full((1, LANES)),
            _full((1, LANES)),
            _full((1, D_INNER)),
            _full((1, D_INNER)),
            once((2 * LANES, D_INNER)),
            once((2 * LANES, SSD_HEADS * D_STATE)),
        ],
        out_specs=[
            pl.BlockSpec((None, rows, D_INNER), lambda b, i: (b, i, 0)),
            pl.BlockSpec((None, SUBLANES, CONV_DIM), lambda b, i: (b, 0, 0)),
            pl.BlockSpec((None, D_INNER, D_STATE), lambda b, i: (b, 0, 0)),
        ],
        out_shape=[
            jax.ShapeDtypeStruct((bt, t, D_INNER), BF16),
            jax.ShapeDtypeStruct((bt, SUBLANES, CONV_DIM), F32),
            jax.ShapeDtypeStruct((bt, D_INNER, D_STATE), F32),
        ],
        scratch_shapes=[
            pltpu.VMEM((SUBLANES + rows, CONV_DIM), F32),
            pltpu.VMEM((rows, CONV_DIM), F32),
            pltpu.VMEM((3 * CHUNK, D_INNER), F32),
            pltpu.VMEM((CHUNK, SSD_HEADS * D_STATE), F32),
        ],
        compiler_params=pltpu.CompilerParams(
            dimension_semantics=("parallel", "arbitrary"), vmem_limit_bytes=VMEM_LIMIT),
        name="ssd_branch",
    )(x, conv_state8, ssm0, w["w_z"], w["w_xbc"], w["w_dt"], w["conv_w"], w["conv_b"], w["dt_bias"],
      w["a_log"], w["d_skip"], w["ssd_norm_g"], _head_spread_matrix(SSD_HEAD_DIM),
      _head_spread_matrix(D_STATE))


def _qkv_kernel(x_ref, w_ref, q_ref, k_ref, v_ref, kb_ref, vb_ref):
    xb = x_ref[...].astype(BF16)
    q_ref[...] = (_dot(xb, w_ref[:, 0:SB_WIDTH]) * (SB_SCALE * LOG2E)).astype(BF16)
    k = _dot(xb, w_ref[:, SB_WIDTH:2 * SB_WIDTH])
    k_ref[...] = k
    kb_ref[...] = k.astype(BF16)
    v = _dot(xb, w_ref[:, 2 * SB_WIDTH:3 * SB_WIDTH])
    v_ref[...] = v
    vb_ref[...] = v.astype(BF16)


def _qkv_proj(x2d, w_qkv, tm):
    t = x2d.shape[0]
    row = lambda i: (i, 0)
    return pl.pallas_call(
        _qkv_kernel,
        grid=(t // tm,),
        in_specs=[pl.BlockSpec((tm, D_MODEL), row), _full((D_MODEL, 3 * SB_WIDTH))],
        out_specs=[pl.BlockSpec((tm, SB_WIDTH), row)] * 5,
        out_shape=[
            jax.ShapeDtypeStruct((t, SB_WIDTH), BF16),
            jax.ShapeDtypeStruct((t, SB_WIDTH), F32),
            jax.ShapeDtypeStruct((t, SB_WIDTH), F32),
            jax.ShapeDtypeStruct((t, SB_WIDTH), BF16),
            jax.ShapeDtypeStruct((t, SB_WIDTH), BF16),
        ],
        compiler_params=pltpu.CompilerParams(
            dimension_semantics=("parallel",), vmem_limit_bytes=VMEM_LIMIT),
        name="qkv_proj",
    )(x2d, w_qkv)


def _qkv_prompt_kernel(x_ref, w_ref, q_ref, kt_ref, vt_ref, ktb_ref, vst_ref):
    xb = x_ref[...].astype(BF16)
    t = xb.shape[0]
    q_ref[...] = (_dot(xb, w_ref[:, 0:SB_WIDTH]) * (SB_SCALE * LOG2E)).astype(BF16)
    k_t = _dot(xb, w_ref[:, SB_WIDTH:2 * SB_WIDTH]).T
    kt_ref[...] = k_t
    ktb_ref[...] = k_t.astype(BF16)
    v = _dot(xb, w_ref[:, 2 * SB_WIDTH:3 * SB_WIDTH])
    vt_ref[...] = v.T
    lo_half = lax.broadcasted_iota(jnp.int32, (t, LANES), 1) < SB_HEAD_DIM
    for hp in range(SB_WIDTH // LANES):
        v2 = v[:, LANES * hp:LANES * (hp + 1)]
        vst_ref[hp, 0:t, :] = jnp.where(lo_half, v2, 0.0).astype(BF16)
        vst_ref[hp, t:2 * t, :] = jnp.where(lo_half, 0.0, v2).astype(BF16)


def _qkv_prompt(x, w_qkv):
    b, s, _ = x.shape
    t = SB_TILE
    n_hp = SB_WIDTH // LANES
    return pl.pallas_call(
        _qkv_prompt_kernel,
        grid=(b, s // t),
        in_specs=[pl.BlockSpec((None, t, D_MODEL), lambda bi, i: (bi, i, 0)),
                  _full((D_MODEL, 3 * SB_WIDTH))],
        out_specs=[
            pl.BlockSpec((None, t, SB_WIDTH), lambda bi, i: (bi, i, 0)),
            pl.BlockSpec((None, SB_WIDTH, t), lambda bi, i: (bi, 0, i)),
            pl.BlockSpec((None, SB_WIDTH, t), lambda bi, i: (bi, 0, i)),
            pl.BlockSpec((None, SB_WIDTH, t), lambda bi, i: (bi, 0, i)),
            pl.BlockSpec((None, None, n_hp, 2 * t, LANES), lambda bi, i: (bi, i, 0, 0, 0)),
        ],
        out_shape=[
            jax.ShapeDtypeStruct((b, s, SB_WIDTH), BF16),
            jax.ShapeDtypeStruct((b, SB_WIDTH, s), F32),
            jax.ShapeDtypeStruct((b, SB_WIDTH, s), F32),
            jax.ShapeDtypeStruct((b, SB_WIDTH, s), BF16),
            jax.ShapeDtypeStruct((b, s // t, n_hp, 2 * t, LANES), BF16),
        ],
        compiler_params=pltpu.CompilerParams(
            dimension_semantics=("parallel", "parallel"), vmem_limit_bytes=VMEM_LIMIT),
        name="qkv_prompt",
    )(x, w_qkv)


def _suffix_matrix(t):
    r = lax.broadcasted_iota(jnp.int32, (t, t), 0)
    c = lax.broadcasted_iota(jnp.int32, (t, t), 1)
    return (r > c).astype(BF16)


def _clamp_logits(z2):
    return jnp.minimum(z2, Z2_MAX)


def _sp2(z2, valid):
    sp = jnp.log(1.0 + jnp.exp2(z2)) * LOG2E
    return sp if valid is None else jnp.where(valid, sp, 0.0)


def _sb_tile_sums(z2, u, valid):
    z2 = _clamp_logits(z2)
    sp = _sp2(z2, valid)
    return z2, sp, _dot(sp.astype(BF16), u)


def _sb_tile_weights(z2, sp, suffix, valid):
    a = jnp.exp2(z2 - sp - suffix)
    if valid is not None:
        a = jnp.where(valid, a, 0.0)
    return a.astype(BF16), jnp.sum(sp, axis=-1, keepdims=True)


def _sb_tile(z2, u, valid):
    return _sb_tile_weights(*_sb_tile_sums(z2, u, valid), valid)


def _sbp_kernel(bias_ref, q_ref, k_ref, v_ref, u_ref, o_ref):
    _sbp_body(pl.program_id(1), pl.program_id(2), bias_ref, q_ref, k_ref, v_ref, u_ref, o_ref)


def _sbp_body(hp, qi, bias_ref, q_ref, k_ref, v_ref, u_ref, o_ref, after_diagonal=None):
    t = SB_TILE
    n_sub = q_ref.shape[0] // t
    lane = lax.broadcasted_iota(jnp.int32, (t, LANES), 1)
    lo_half = lane < SB_HEAD_DIM

    def bias_lanes(b):
        bv = jnp.full((t, LANES), b * LOG2E, F32)
        hi = bv.astype(BF16).astype(F32)
        return jnp.where(lane == 0, hi, jnp.where(lane == 1, bv - hi, 0.0)).astype(BF16)

    ext = jnp.concatenate([bias_lanes(bias_ref[2 * hp]), bias_lanes(bias_ref[2 * hp + 1])], axis=0)
    qs = []
    for s in range(n_sub):
        q2 = q_ref[s * t:(s + 1) * t, :].astype(F32)
        q_heads = jnp.concatenate([jnp.where(lo_half, q2, 0.0), jnp.where(lo_half, 0.0, q2)],
                                  axis=0).astype(BF16)
        qs.append(jnp.concatenate([q_heads, ext], axis=1))
    ones_rows = (lax.broadcasted_iota(jnp.int32, (LANES, t), 0) < 2).astype(BF16)
    u = u_ref[...]
    r = lax.broadcasted_iota(jnp.int32, (t, t), 0)
    c = lax.broadcasted_iota(jnp.int32, (t, t), 1)
    diag_valid = c < r

    def sums(z, valid):
        return _sb_tile_sums(z[0:t], u, valid), _sb_tile_sums(z[t:2 * t], u, valid)

    def finish(halves, ls2, v_tile, valid):
        a0, rs0 = _sb_tile_weights(*halves[0], valid)
        a1, rs1 = _sb_tile_weights(*halves[1], valid)
        out = _dot(jnp.concatenate([a0, a1], axis=1), v_tile)
        return out * jnp.exp2(-ls2), ls2 + jnp.where(lo_half, rs0, rs1)

    def tile(kt, carry, valids):
        accs, lss = list(carry[0]), list(carry[1])
        k0 = pl.multiple_of(kt * t, t)
        v_tile = v_ref[kt]
        live = [s for s in range(n_sub) if not isinstance(valids[s], str)]
        k_ext = jnp.concatenate([k_ref[:, pl.ds(k0, t)], ones_rows], axis=0)
        z_all = _dot(jnp.concatenate([qs[s] for s in live], axis=0), k_ext)
        z_of = lambda i: z_all[2 * t * i:2 * t * (i + 1)]
        ahead = sums(z_of(0), valids[live[0]])
        for i, s in enumerate(live):
            halves = ahead
            if i + 1 < len(live):
                ahead = sums(z_of(i + 1), valids[live[i + 1]])
            d, lss[s] = finish(halves, lss[s], v_tile, valids[s])
            accs[s] = accs[s] + d
        return tuple(accs), tuple(lss)

    carry = (tuple(jnp.zeros((t, LANES), F32) for _ in range(n_sub)),
             tuple(jnp.zeros((t, LANES), F32) for _ in range(n_sub)))
    for d in range(n_sub - 1, -1, -1):
        valids = ["skip" if s < d else (diag_valid if s == d else None) for s in range(n_sub)]
        carry = tile(n_sub * qi + d, carry, valids)
    if after_diagonal is not None:
        after_diagonal()
    n_full = n_sub * qi
    accs, _ = lax.fori_loop(0, n_full, lambda i, cr: tile(n_full - 1 - i, cr, [None] * n_sub), carry)
    for s in range(n_sub):
        o_ref[s * t:(s + 1) * t, :] = accs[s].astype(o_ref.dtype)


def _sb_prompt(q, ktb, vst, sb_bias):
    b, s, _ = q.shape
    t = SB_TILE
    tq = SB_Q_SUBBLOCKS * t
    n_hp = SB_WIDTH // LANES
    grid_spec = pltpu.PrefetchScalarGridSpec(
        num_scalar_prefetch=1,
        grid=(b, n_hp, s // tq),
        in_specs=[
            pl.BlockSpec((None, tq, LANES), lambda bi, hp, qi, bias: (bi, qi, hp)),
            pl.BlockSpec((None, LANES, s), lambda bi, hp, qi, bias: (bi, hp, 0)),
            pl.BlockSpec((None, s // t, None, 2 * t, LANES), lambda bi, hp, qi, bias: (bi, 0, hp, 0, 0)),
            pl.BlockSpec((t, t), lambda bi, hp, qi, bias: (0, 0)),
        ],
        out_specs=pl.BlockSpec((None, tq, LANES), lambda bi, hp, qi, bias: (bi, qi, hp)),
    )
    return pl.pallas_call(
        _sbp_kernel,
        grid_spec=grid_spec,
        out_shape=jax.ShapeDtypeStruct((b, s, SB_WIDTH), BF16),
        compiler_params=pltpu.CompilerParams(
            dimension_semantics=("parallel", "parallel", "arbitrary"), vmem_limit_bytes=VMEM_LIMIT),
        name="sb_prompt",
    )(sb_bias, q, ktb, vst, _suffix_matrix(t))


def _sbs_kernel(pt_ref, q_ref, kn_ref, vn_ref, bias_ref, u_ref, *rest, n_tok, n_pp):
    kp_refs, vp_refs = rest[0:n_pp], rest[n_pp:2 * n_pp]
    o_ref, acc, ls = rest[2 * n_pp:]
    _sbs_body(pl.program_id(1), pl.num_programs(1), q_ref, kn_ref, vn_ref, bias_ref, u_ref, kp_refs, vp_refs,
              o_ref, acc, ls, n_tok)


def _sbs_body(j, n_j, q_ref, kn_ref, vn_ref, bias_ref, u_ref, kp_refs, vp_refs, o_ref, acc, ls, n_tok,
              place_sweep=None):
    n_pp = len(kp_refs)
    n_rows = q_ref.shape[0]
    row = lax.broadcasted_iota(jnp.int32, (n_rows, SB_WIDTH), 0)
    lane = lax.broadcasted_iota(jnp.int32, (n_rows, SB_WIDTH), 1)
    head_mask = (row & (SB_HEADS - 1)) == (lane >> 6)
    q_bd = jnp.where(head_mask, q_ref[...].astype(F32), 0.0).astype(BF16)
    bias2 = bias_ref[...] * LOG2E
    u2 = u_ref[...]

    @pl.when(j == 0)
    def _():
        tok = lax.broadcasted_iota(jnp.int32, (n_rows, PAGE), 0) >> 4
        key = lax.broadcasted_iota(jnp.int32, (n_rows, PAGE), 1)
        a, rs = _sb_tile(_dot_nt(q_bd, kn_ref[...]) + bias2, u2[0:PAGE, 0:PAGE], key < tok)
        acc[...] = _dot(a, vn_ref[...])
        ls[...] = rs

    def sweep():
        k_cat = jnp.concatenate([r[...].astype(BF16) for r in kp_refs], axis=1)
        z = _clamp_logits(_dot(q_bd, k_cat) + jnp.concatenate([bias2] * n_pp, axis=1))
        sp = _sp2(z, None)
        off = ls[...]
        a_parts = []
        for pr in range(n_pp // 2):
            sl = slice(2 * PAGE * pr, 2 * PAGE * (pr + 1))
            zs = z[:, sl] - sp[:, sl] - _dot(sp[:, sl].astype(BF16), u2)
            for p in range(2):
                ps = slice(PAGE * p, PAGE * (p + 1))
                a_parts.append(jnp.exp2(zs[:, ps] - off).astype(BF16))
                off = off + jnp.sum(sp[:, sl][:, ps], axis=-1, keepdims=True)
        v_cat = jnp.concatenate([r[...].astype(BF16) for r in vp_refs], axis=1)
        acc[...] += _dot_nt(jnp.concatenate(a_parts, axis=1), v_cat)
        ls[...] = off

    if place_sweep is None:
        sweep()
    else:
        place_sweep(sweep)

    @pl.when(j == n_j - 1)
    def _():
        own = jnp.where(head_mask, acc[...], 0.0)
        o_ref[...] = own.reshape(n_tok, SB_HEADS, SB_WIDTH).sum(axis=1)


def _pages_per_step(n_pages):
    for p in range(MAX_PAGES_PER_STEP, 0, -2):
        if n_pages % p == 0:
            return p
    raise ValueError("the page count must be even")


def _sample_operands(q, k_new, v_new, sb_bias):
    n_tok = q.shape[1]
    n_rows = n_tok * SB_HEADS
    q_rep = jnp.repeat(q, SB_HEADS, axis=1)
    pad = ((0, 0), (0, PAGE - n_tok), (0, 0))
    bias_rows = jnp.broadcast_to(jnp.tile(sb_bias, n_tok)[:, None], (n_rows, PAGE)).astype(F32)
    u1 = _suffix_matrix(PAGE)
    zero = jnp.zeros_like(u1)
    u2 = jnp.concatenate([jnp.concatenate([u1, zero], axis=1), jnp.concatenate([zero, u1], axis=1)], axis=0)
    return q_rep, jnp.pad(k_new, pad), jnp.pad(v_new, pad), bias_rows, u2


def _sb_fused_kernel(bias_ref, pt_ref, q_ref, k_ref, v_ref, u_ref, qs_ref, kn_ref, vn_ref, brow_ref, u2_ref,
                     *rest, n_tok, n_pp, steps_per_batch):
    kp_refs, vp_refs = rest[0:n_pp], rest[n_pp:2 * n_pp]
    o_ref, os_ref, acc, ls = rest[2 * n_pp:]
    step = (pl.program_id(0) * pl.num_programs(1) + pl.program_id(1)) * pl.num_programs(2) + pl.program_id(2)
    hp, qi = pl.program_id(1), pl.program_id(2)
    _sbs_body(lax.rem(step, steps_per_batch), steps_per_batch, qs_ref, kn_ref, vn_ref, brow_ref, u2_ref,
              kp_refs, vp_refs, os_ref, acc, ls, n_tok,
              place_sweep=lambda sweep: _sbp_body(hp, qi, bias_ref, q_ref, k_ref, v_ref, u_ref, o_ref,
                                                  after_diagonal=sweep))


def _fused_pages_per_step(n_prompt_steps, bt, n_pages):
    total = bt * n_pages
    if total % n_prompt_steps:
        return None
    n_pp = total // n_prompt_steps
    return n_pp if (n_pp % 2 == 0 and n_pages % n_pp == 0) else None


def _sb_fused(q, ktb, vst, sb_bias, qs, k_new, v_new, cache_kt, cache_vt, page_table, n_pp):
    b, s, _ = q.shape
    t = SB_TILE
    tq = SB_Q_SUBBLOCKS * t
    n_hp = SB_WIDTH // LANES
    n_q = s // tq
    bt, n_tok, _ = qs.shape
    n_pages = page_table.shape[1]
    spb = n_pages // n_pp
    n_rows = n_tok * SB_HEADS
    q_rep, kn, vn, bias_rows, u2 = _sample_operands(qs, k_new, v_new, sb_bias)
    step = lambda bi, hp, qi: (bi * n_hp + hp) * n_q + qi
    per_sb = lambda bi, hp, qi, bias, pt: (step(bi, hp, qi) // spb, 0, 0)
    const = lambda bi, hp, qi, bias, pt: (0, 0)

    def page(p):
        def index(bi, hp, qi, bias, pt):
            st = step(bi, hp, qi)
            return (pt[st // spb, n_pages - 1 - ((st % spb) * n_pp + p)], 0, 0)
        return index

    page_specs = [pl.BlockSpec((None, SB_WIDTH, PAGE), page(p)) for p in range(n_pp)]
    grid_spec = pltpu.PrefetchScalarGridSpec(
        num_scalar_prefetch=2,
        grid=(b, n_hp, n_q),
        in_specs=[
            pl.BlockSpec((None, tq, LANES), lambda bi, hp, qi, bias, pt: (bi, qi, hp)),
            pl.BlockSpec((None, LANES, s), lambda bi, hp, qi, bias, pt: (bi, hp, 0)),
            pl.BlockSpec((None, s // t, None, 2 * t, LANES), lambda bi, hp, qi, bias, pt: (bi, 0, hp, 0, 0)),
            pl.BlockSpec((t, t), const),
            pl.BlockSpec((None, n_rows, SB_WIDTH), per_sb),
            pl.BlockSpec((None, PAGE, SB_WIDTH), per_sb),
            pl.BlockSpec((None, PAGE, SB_WIDTH), per_sb),
            pl.BlockSpec((n_rows, PAGE), const),
            pl.BlockSpec((2 * PAGE, 2 * PAGE), const),
        ] + page_specs + page_specs,
        out_specs=[
            pl.BlockSpec((None, tq, LANES), lambda bi, hp, qi, bias, pt: (bi, qi, hp)),
            pl.BlockSpec((None, n_tok, SB_WIDTH), per_sb),
        ],
        scratch_shapes=[pltpu.VMEM((n_rows, SB_WIDTH), F32), pltpu.VMEM((n_rows, 1), F32)],
    )
    return pl.pallas_call(
        functools.partial(_sb_fused_kernel, n_tok=n_tok, n_pp=n_pp, steps_per_batch=spb),
        grid_spec=grid_spec,
        out_shape=[jax.ShapeDtypeStruct((b, s, SB_WIDTH), BF16),
                   jax.ShapeDtypeStruct((bt, n_tok, SB_WIDTH), F32)],
        compiler_params=pltpu.CompilerParams(
            dimension_semantics=("arbitrary", "arbitrary", "arbitrary"), vmem_limit_bytes=VMEM_LIMIT),
        name="sb_fused",
    )(sb_bias, page_table, q, ktb, vst, _suffix_matrix(t), q_rep, kn, vn, bias_rows, u2,
      *([cache_kt] * n_pp), *([cache_vt] * n_pp))


def _sb_sample(q, k_new, v_new, cache_kt, cache_vt, page_table, sb_bias):
    bt, n_tok, _ = q.shape
    n_pages = page_table.shape[1]
    n_pp = _pages_per_step(n_pages)
    n_rows = n_tok * SB_HEADS
    q_rep, kn, vn, bias_rows, u2 = _sample_operands(q, k_new, v_new, sb_bias)
    per_b = lambda b, j, pt: (b, 0, 0)
    const = lambda b, j, pt: (0, 0)

    def page(p):
        return lambda b, j, pt: (pt[b, n_pages - 1 - (j * n_pp + p)], 0, 0)

    page_specs = [pl.BlockSpec((None, SB_WIDTH, PAGE), page(p)) for p in range(n_pp)]
    grid_spec = pltpu.PrefetchScalarGridSpec(
        num_scalar_prefetch=1,
        grid=(bt, n_pages // n_pp),
        in_specs=[
            pl.BlockSpec((None, n_rows, SB_WIDTH), per_b),
            pl.BlockSpec((None, PAGE, SB_WIDTH), per_b),
            pl.BlockSpec((None, PAGE, SB_WIDTH), per_b),
            pl.BlockSpec((n_rows, PAGE), const),
            pl.BlockSpec((2 * PAGE, 2 * PAGE), const),
        ] + page_specs + page_specs,
        out_specs=pl.BlockSpec((None, n_tok, SB_WIDTH), per_b),
        scratch_shapes=[pltpu.VMEM((n_rows, SB_WIDTH), F32), pltpu.VMEM((n_rows, 1), F32)],
    )
    return pl.pallas_call(
        functools.partial(_sbs_kernel, n_tok=n_tok, n_pp=n_pp),
        grid_spec=grid_spec,
        out_shape=jax.ShapeDtypeStruct((bt, n_tok, SB_WIDTH), F32),
        compiler_params=pltpu.CompilerParams(
            dimension_semantics=("parallel", "arbitrary"), vmem_limit_bytes=VMEM_LIMIT),
        name="sb_sample",
    )(page_table, q_rep, kn, vn, bias_rows, u2, *([cache_kt] * n_pp), *([cache_vt] * n_pp))


def _merge_kernel(x_ref, y_ref, o_ref, wg_ref, wso_ref, wbo_ref, wo_ref, g_ref, b_ref, h_ref):
    x = x_ref[...]
    xb = x.astype(BF16)
    ga = _dot(xb, wg_ref[:, 0:D_MODEL])
    gb = _dot(xb, wg_ref[:, D_MODEL:2 * D_MODEL])
    p_ssd = _dot(y_ref[...].astype(BF16), wso_ref[...])
    p_sb = _dot(o_ref[...].astype(BF16), wbo_ref[...])
    merged = jax.nn.sigmoid(ga) * p_ssd + jax.nn.sigmoid(gb) * p_sb
    pre = DN_ALPHA * x + _dot(merged.astype(BF16), wo_ref[...])
    h_ref[...] = _layer_norm(pre, g_ref[...], b_ref[...])


def _merge_ln(x2d, y_ssd, o_sb, w, tm):
    t = x2d.shape[0]
    row = lambda i: (i, 0)
    return pl.pallas_call(
        _merge_kernel,
        grid=(t // tm,),
        in_specs=[
            pl.BlockSpec((tm, D_MODEL), row),
            pl.BlockSpec((tm, D_INNER), row),
            pl.BlockSpec((tm, SB_WIDTH), row),
            _full((D_MODEL, 2 * D_MODEL)),
            _full((D_INNER, D_MODEL)),
            _full((SB_WIDTH, D_MODEL)),
            _full((D_MODEL, D_MODEL)),
            _full((1, D_MODEL)),
            _full((1, D_MODEL)),
        ],
        out_specs=pl.BlockSpec((tm, D_MODEL), row),
        out_shape=jax.ShapeDtypeStruct((t, D_MODEL), F32),
        compiler_params=pltpu.CompilerParams(
            dimension_semantics=("parallel",), vmem_limit_bytes=VMEM_LIMIT),
        name="merge_ln",
    )(x2d, y_ssd, o_sb, w["w_gate"], w["w_ssd_out"], w["w_sb_out"], w["w_out"], w["ln1_g"], w["ln1_b"])


def _ffn_kernel(h_ref, pe_ref, wgu_ref, wd_ref, wpg_ref, wpp_ref, g_ref, b_ref, o_ref):
    h = h_ref[...]
    hb = h.astype(BF16)
    g_ = _dot(hb, wgu_ref[:, 0:D_FF])
    u_ = _dot(hb, wgu_ref[:, D_FF:2 * D_FF])
    ffn = _dot((_silu(g_) * u_).astype(BF16), wd_ref[...])
    ple = jax.nn.sigmoid(_dot(hb, wpg_ref[...])) * _dot(pe_ref[...].astype(BF16), wpp_ref[...])
    o_ref[...] = _layer_norm(DN_ALPHA * h + ffn + ple, g_ref[...], b_ref[...])


def _ffn_ln(h2d, pe2d, w, tm):
    t = h2d.shape[0]
    row = lambda i: (i, 0)
    once = pl.Buffered(1)
    return pl.pallas_call(
        _ffn_kernel,
        grid=(t // tm,),
        in_specs=[
            pl.BlockSpec((tm, D_MODEL), row),
            pl.BlockSpec((tm, PLE_DIM), row),
            pl.BlockSpec((D_MODEL, 2 * D_FF), lambda i: (0, 0), pipeline_mode=once),
            pl.BlockSpec((D_FF, D_MODEL), lambda i: (0, 0), pipeline_mode=once),
            pl.BlockSpec((D_MODEL, D_MODEL), lambda i: (0, 0), pipeline_mode=once),
            pl.BlockSpec((PLE_DIM, D_MODEL), lambda i: (0, 0), pipeline_mode=once),
            _full((1, D_MODEL)),
            _full((1, D_MODEL)),
        ],
        out_specs=pl.BlockSpec((tm, D_MODEL), row),
        out_shape=jax.ShapeDtypeStruct((t, D_MODEL), F32),
        compiler_params=pltpu.CompilerParams(
            dimension_semantics=("parallel",), vmem_limit_bytes=VMEM_LIMIT),
        name="ffn_ln",
    )(h2d, pe2d, w["w_gu"], w["w_down"], w["w_ple_gate"], w["w_ple_proj"], w["ln2_g"], w["ln2_b"])


def _prep_weights(w_in, conv_w, conv_b, dt_bias, a_log, d_skip, ssd_norm_g, w_ssd_out, w_sb_out, w_out,
                  ln1_g, ln1_b, w_gu, w_down, w_ple_gate, w_ple_proj, ln2_g, ln2_b):
    offs = [0]
    for s in _IN_SIZES:
        offs.append(offs[-1] + s)
    seg = lambda k: w_in[:, offs[k]:offs[k + 1]]
    lane_pad = lambda a: jnp.pad(a, ((0, 0), (0, LANES - a.shape[1])))
    return dict(
        w_z=seg(0).astype(BF16),
        w_xbc=seg(1).astype(BF16),
        w_dt=lane_pad(seg(2)).astype(BF16),
        w_qkv=jnp.concatenate([seg(3), seg(4), seg(5)], axis=1).astype(BF16),
        w_gate=jnp.concatenate([seg(6), seg(7)], axis=1).astype(BF16),
        conv_w=conv_w, conv_b=conv_b[None, :],
        dt_bias=lane_pad(dt_bias[None, :]), a_log=lane_pad(a_log[None, :]),
        d_skip=jnp.repeat(d_skip, SSD_HEAD_DIM)[None, :], ssd_norm_g=ssd_norm_g[None, :],
        w_ssd_out=w_ssd_out.astype(BF16), w_sb_out=w_sb_out.astype(BF16), w_out=w_out.astype(BF16),
        ln1_g=ln1_g[None, :], ln1_b=ln1_b[None, :],
        w_gu=w_gu.astype(BF16), w_down=w_down.astype(BF16),
        w_ple_gate=w_ple_gate.astype(BF16), w_ple_proj=w_ple_proj.astype(BF16),
        ln2_g=ln2_g[None, :], ln2_b=ln2_b[None, :],
    )


def _tail(x2d, pe2d, y_ssd2d, o_sb2d, w, tm):
    h = _merge_ln(x2d, y_ssd2d, o_sb2d, w, tm)
    return _ffn_ln(h, pe2d, w, tm)


def _heads_last(a_t, bt, s):
    return jnp.transpose(a_t.reshape(bt, SB_HEADS, SB_HEAD_DIM, s), (0, 3, 1, 2))


def _layer(xp, pp, xs, ps, state_conv, state_ssm, cache_k, cache_v, page_table, w, sb_bias, tm=512):
    bp, s, _ = xp.shape
    bs, n_tok, _ = xs.shape
    tp, ts = bp * s, bs * n_tok
    xp2d, xs2d = xp.reshape(tp, D_MODEL), xs.reshape(ts, D_MODEL)

    yp_ssd, conv_p, ssm_p = _ssd_branch(xp, jnp.zeros((bp, SUBLANES, CONV_DIM), F32),
                                        jnp.zeros((bp, D_INNER, D_STATE), F32), w, CHUNK)
    xs_pad = jnp.pad(xs, ((0, 0), (0, CHUNK - n_tok), (0, 0)))
    conv8 = jnp.pad(state_conv, ((0, 0), (SUBLANES - (CONV_W - 1), 0), (0, 0)))
    ys_pad, conv_s, ssm_s = _ssd_branch(xs_pad, conv8, state_ssm.reshape(bs, D_INNER, D_STATE), w, n_tok)
    ys_ssd = ys_pad[:, :n_tok].reshape(ts, D_INNER)

    qp, kp_t, vp_t, ktb, vst = _qkv_prompt(xp, w["w_qkv"])
    qs, ks, vs, ksb, vsb = _qkv_proj(xs2d, w["w_qkv"], ts)
    shp = (bs, n_tok, SB_WIDTH)
    qs, ksb, vsb = qs.reshape(shp), ksb.reshape(shp), vsb.reshape(shp)
    n_pool = cache_k.shape[0]
    cache_kt = jnp.transpose(cache_k, (0, 2, 3, 1)).reshape(n_pool, SB_WIDTH, PAGE)
    cache_vt = jnp.transpose(cache_v, (0, 2, 3, 1)).reshape(n_pool, SB_WIDTH, PAGE)
    n_prompt_steps = bp * (SB_WIDTH // LANES) * (s // (SB_Q_SUBBLOCKS * SB_TILE))
    n_pp = _fused_pages_per_step(n_prompt_steps, bs, page_table.shape[1])
    if n_pp is None:
        op_sb = _sb_prompt(qp, ktb, vst, sb_bias)
        os_sb = _sb_sample(qs, ksb, vsb, cache_kt, cache_vt, page_table, sb_bias)
    else:
        op_sb, os_sb = _sb_fused(qp, ktb, vst, sb_bias, qs, ksb, vsb, cache_kt, cache_vt, page_table, n_pp)

    out_p = _tail(xp2d, pp.reshape(tp, PLE_DIM), yp_ssd.reshape(tp, D_INNER), op_sb.reshape(tp, SB_WIDTH), w, tm)
    out_s = _tail(xs2d, ps.reshape(ts, PLE_DIM), ys_ssd, os_sb.reshape(ts, SB_WIDTH), w, ts)
    tail_rows = slice(SUBLANES - (CONV_W - 1), SUBLANES)
    heads = lambda a, bt, n: a.reshape(bt, n, SB_HEADS, SB_HEAD_DIM)
    return (out_p.reshape(bp, s, D_MODEL), out_s.reshape(bs, n_tok, D_MODEL),
            _heads_last(kp_t, bp, s), _heads_last(vp_t, bp, s), conv_p[:, tail_rows],
            ssm_p.reshape(bp, SSD_HEADS, SSD_HEAD_DIM, D_STATE),
            heads(ks, bs, n_tok), heads(vs, bs, n_tok), conv_s[:, tail_rows],
            ssm_s.reshape(bs, SSD_HEADS, SSD_HEAD_DIM, D_STATE))


def kernel(x_prompt, x_sample, cache_k, cache_v, state_conv, state_ssm, page_table, p_prompt, p_sample,
           w_in, conv_w, conv_b, dt_bias, a_log, d_skip, ssd_norm_g, sb_bias, w_ssd_out, w_sb_out, w_out,
           ln1_g, ln1_b, w_gu, w_down, w_ple_gate, w_ple_proj, ln2_g, ln2_b):
    assert w_in.shape[0] == DEPTH
    w = _prep_weights(w_in[0], conv_w[0], conv_b[0], dt_bias[0], a_log[0], d_skip[0], ssd_norm_g[0],
                      w_ssd_out[0], w_sb_out[0], w_out[0], ln1_g[0], ln1_b[0], w_gu[0], w_down[0],
                      w_ple_gate[0], w_ple_proj[0], ln2_g[0], ln2_b[0])
    outs = _layer(x_prompt, p_prompt[0], x_sample, p_sample[0], state_conv[0], state_ssm[0],
                  cache_k[0], cache_v[0], page_table, w, sb_bias[0])
    yp, ys = outs[0], outs[1]
    return (yp, ys) + tuple(o[None] for o in outs[2:])
```
